```python
import math
import jax, jax.numpy as jnp
from jax import lax
import numpy as np

D_MODEL = 1024
BATCH = 4
SEQ = 8192
DEPTH = 1

D_MIX = D_MODEL
RET_WIDTH = D_MIX // 2
RET_HEADS = 4
RET_HEAD_DIM = RET_WIDTH // RET_HEADS
RET_CHUNK = 128
ROPE_BASE = 10000.0
SSM_WIDTH = D_MIX - RET_WIDTH
SSM_GROUP = 16
SSM_GROUPS = SSM_WIDTH // SSM_GROUP
SSM_STATE = 64
DT_MIN = 1e-3
DT_MAX = 1e-1
D_FF = 4 * D_MODEL
NORM_EPS = 1e-6
IN_COLS = 4 * RET_WIDTH + SSM_WIDTH

kernel_name = "hymba_retnet_s5_sandwich_block"


def rmsnorm(x, g):
    xf = x.astype(jnp.float32)
    y = xf * lax.rsqrt(jnp.mean(xf * xf, axis=-1, keepdims=True) + NORM_EPS) * g.astype(jnp.float32)
    return y.astype(x.dtype)


def rope(x):
    L, d = x.shape[1], x.shape[-1]
    half = d // 2
    inv_freq = ROPE_BASE ** (-jnp.arange(half, dtype=jnp.float32) / half)
    ang = jnp.arange(L, dtype=jnp.float32)[:, None] * inv_freq[None, :]
    cos = jnp.cos(ang)[None, :, None, :]
    sin = jnp.sin(ang)[None, :, None, :]
    x1, x2 = x[..., :half], x[..., half:]
    return jnp.concatenate([x1 * cos - x2 * sin, x1 * sin + x2 * cos], axis=-1)


def retention_chunkwise(q, k, v):
    B, L, H, d = q.shape
    C = RET_CHUNK
    nc = L // C
    log_gamma = jnp.log(1.0 - jnp.exp(jnp.linspace(math.log(1.0 / 32), math.log(1.0 / 512), H))).astype(jnp.float32)
    q = q.reshape(B, nc, C, H, d)
    k = k.reshape(B, nc, C, H, d)
    v = v.reshape(B, nc, C, H, d)
    idx = jnp.arange(C, dtype=jnp.float32)
    diff = idx[:, None] - idx[None, :]
    decay = jnp.where(diff[None] >= 0, jnp.exp(jnp.maximum(diff, 0.0)[None] * log_gamma[:, None, None]), 0.0)
    s = jnp.einsum('bnihk,bnjhk->bnhij', q, k) * decay[None, None]
    inner = jnp.einsum('bnhij,bnjhd->bnihd', s, v)
    zeta = jnp.exp((C - 1 - idx)[None, :] * log_gamma[:, None])
    S = jnp.einsum('bnjhk,bnjhd,hj->bnhkd', k, v, zeta)
    g_chunk = jnp.exp(C * log_gamma)[None, :, None, None]

    def step(R, S_i):
        return g_chunk * R + S_i, R

    R0 = jnp.zeros((B, H, d, d), jnp.float32)
    _, R_prev = lax.scan(step, R0, jnp.moveaxis(S, 1, 0))
    R_prev = jnp.moveaxis(R_prev, 0, 1)
    xi = jnp.exp((idx + 1.0)[None, :] * log_gamma[:, None])
    cross = jnp.einsum('bnihk,bnhkd,hi->bnihd', q, R_prev, xi)
    return (inner + cross).reshape(B, L, H, d)


def head_groupnorm(y, g):
    mu = jnp.mean(y, axis=-1, keepdims=True)
    var = jnp.mean(jnp.square(y - mu), axis=-1, keepdims=True)
    yn = (y - mu) * lax.rsqrt(var + NORM_EPS)
    return yn * g.astype(jnp.float32).reshape(RET_HEADS, RET_HEAD_DIM)


def s5_scan(u, lam_re, lam_im, log_dt, b_re, b_im, c_re, c_im, d_skip):
    B, L, _ = u.shape
    uf = u.astype(jnp.float32).reshape(B, L, SSM_GROUPS, SSM_GROUP)
    lam = lax.complex(jnp.minimum(lam_re.astype(jnp.float32), -1e-4), lam_im.astype(jnp.float32))
    dt = jnp.exp(log_dt.astype(jnp.float32))[:, None]
    lam_bar = jnp.exp(lam * dt)
    b_c = lax.complex(b_re.astype(jnp.float32), b_im.astype(jnp.float32))
    b_bar = ((lam_bar - 1.0) / lam)[:, :, None] * b_c
    bu = jnp.einsum('blgc,gpc->blgp', uf.astype(jnp.complex64), b_bar)
    a = jnp.broadcast_to(lam_bar, bu.shape)

    def combine(e1, e2):
        a1, x1 = e1
        a2, x2 = e2
        return a2 * a1, a2 * x1 + x2

    _, states = lax.associative_scan(combine, (a, bu), axis=1)
    c_c = lax.complex(c_re.astype(jnp.float32), c_im.astype(jnp.float32))
    y = jnp.real(jnp.einsum('blgp,gcp->blgc', states, c_c))
    y = y + d_skip.astype(jnp.float32).reshape(SSM_GROUPS, SSM_GROUP) * uf
    return y.reshape(B, L, SSM_WIDTH)


def setup_inputs(seed: int = 0) -> dict:
    key = jax.random.key(seed)
    ks = jax.random.split(key, 20)
    f32 = jnp.float32
    nrm = lambda k, shape, scale: (jax.random.normal(k, shape, f32) * scale)
    gain = lambda k, shape: 1.0 + 0.02 * jax.random.normal(k, shape, f32)
    x = jax.random.normal(ks[0], (BATCH, SEQ, D_MODEL), f32)
    lam_im_base = math.pi * jnp.arange(SSM_STATE, dtype=f32)
    return {
        "x": x,
        "norm_mix_pre": gain(ks[1], (DEPTH, D_MODEL)),
        "norm_mix_post": gain(ks[2], (DEPTH, D_MODEL)),
        "w_in": nrm(ks[3], (DEPTH, D_MODEL, IN_COLS), D_MODEL ** -0.5),
        "ret_gn_gain": gain(ks[4], (DEPTH, RET_WIDTH)),
        "ssm_lambda_re": -0.5 + 0.01 * jax.random.normal(ks[5], (DEPTH, SSM_GROUPS, SSM_STATE), f32),
        "ssm_lambda_im": lam_im_base + 0.01 * jax.random.normal(ks[6], (DEPTH, SSM_GROUPS, SSM_STATE), f32),
        "ssm_log_dt": jax.random.uniform(ks[7], (DEPTH, SSM_GROUPS), f32, math.log(DT_MIN), math.log(DT_MAX)),
        "ssm_b_re": nrm(ks[8], (DEPTH, SSM_GROUPS, SSM_STATE, SSM_GROUP), (2 * SSM_GROUP) ** -0.5),
        "ssm_b_im": nrm(ks[9], (DEPTH, SSM_GROUPS, SSM_STATE, SSM_GROUP), (2 * SSM_GROUP) ** -0.5),
        "ssm_c_re": nrm(ks[10], (DEPTH, SSM_GROUPS, SSM_GROUP, SSM_STATE), (2 * SSM_STATE) ** -0.5),
        "ssm_c_im": nrm(ks[11], (DEPTH, SSM_GROUPS, SSM_GROUP, SSM_STATE), (2 * SSM_STATE) ** -0.5),
        "ssm_d": nrm(ks[12], (DEPTH, SSM_WIDTH), 1.0),
        "w_glu": nrm(ks[13], (DEPTH, SSM_WIDTH, 2 * SSM_WIDTH), SSM_WIDTH ** -0.5),
        "w_out": nrm(ks[14], (DEPTH, D_MIX, D_MODEL), D_MIX ** -0.5),
        "norm_mlp_pre": gain(ks[15], (DEPTH, D_MODEL)),
        "norm_mlp_post": gain(ks[16], (DEPTH, D_MODEL)),
        "w_ff1": nrm(ks[17], (DEPTH, D_MODEL, D_FF), D_MODEL ** -0.5),
        "w_ff2": nrm(ks[18], (DEPTH, D_FF, D_MODEL), D_FF ** -0.5),
    }


def reference(x, norm_mix_pre, norm_mix_post, w_in, ret_gn_gain, ssm_lambda_re, ssm_lambda_im,
              ssm_log_dt, ssm_b_re, ssm_b_im, ssm_c_re, ssm_c_im, ssm_d, w_glu, w_out,
              norm_mlp_pre, norm_mlp_post, w_ff1, w_ff2):
    B, L, _ = x.shape
    for i in range(DEPTH):
        h = rmsnorm(x, norm_mix_pre[i])
        proj = h @ w_in[i]
        q, k, v, gate, u = jnp.split(proj, [RET_WIDTH, 2 * RET_WIDTH, 3 * RET_WIDTH, 4 * RET_WIDTH], axis=-1)
        heads = lambda t: t.astype(jnp.float32).reshape(B, L, RET_HEADS, RET_HEAD_DIM)
        qh = rope(heads(q))
        kh = rope(heads(k)) * (RET_HEAD_DIM ** -0.5)
        vh = heads(v)
        y_ret = head_groupnorm(retention_chunkwise(qh, kh, vh), ret_gn_gain[i]).reshape(B, L, RET_WIDTH)
        y_ret = (jax.nn.silu(gate.astype(jnp.float32)) * y_ret).astype(x.dtype)

        y_ssm = jax.nn.gelu(s5_scan(u, ssm_lambda_re[i], ssm_lambda_im[i], ssm_log_dt[i], ssm_b_re[i],
                                    ssm_b_im[i], ssm_c_re[i], ssm_c_im[i], ssm_d[i])).astype(x.dtype)
        glu_a, glu_b = jnp.split(y_ssm @ w_glu[i], 2, axis=-1)
        y_ssm = glu_a * jax.nn.sigmoid(glu_b)

        mix = jnp.concatenate([y_ret, y_ssm], axis=-1) @ w_out[i]
        x = x + rmsnorm(mix, norm_mix_post[i])

        h = rmsnorm(x, norm_mlp_pre[i])
        m = jnp.square(jax.nn.relu(h @ w_ff1[i])) @ w_ff2[i]
        x = x + rmsnorm(m, norm_mlp_post[i])
    return x
```

```python
import functools
import math

import jax
import jax.numpy as jnp
from jax import lax
from jax.experimental import pallas as pl
from jax.experimental.pallas import tpu as pltpu

D_MODEL = 1024
RET_WIDTH = 512
RET_HEADS = 4
HEAD_DIM = 128
CHUNK = 128
ROPE_BASE = 10000.0
SSM_WIDTH = 512
SSM_GROUP = 16
SSM_GROUPS = 32
SSM_STATE = 64
D_FF = 4096
NORM_EPS = 1e-6
LANES = 128

BF16 = jnp.bfloat16
F32 = jnp.float32

VMEM_LIMIT_BYTES = 56 * 1024 * 1024


def _dot(a, b):
    return jnp.dot(a, b, preferred_element_type=F32)


def _dot_nt(a, b):
    return lax.dot_general(a, b, (((1,), (1,)), ((), ())), preferred_element_type=F32)


def _dot_tn(a, b):
    return lax.dot_general(a, b, (((0,), (0,)), ((), ())), preferred_element_type=F32)


def _inproj_kernel(x_ref, g_ref, wqvg_ref, wkut_ref, cosf_ref, sinf_ref, cost_ref, sint_ref,
                   q_ref, v_ref, gate_ref, kt_ref, ut_ref):
    x = x_ref[...]
    ms = jnp.mean(x * x, axis=-1, keepdims=True)
    h = (x * lax.rsqrt(ms + NORM_EPS) * g_ref[...]).astype(BF16)
    p = _dot(h, wqvg_ref[...])
    cosf = cosf_ref[...]
    sinf = sinf_ref[...]
    for hd in range(RET_HEADS):
        qh = p[:, hd * HEAD_DIM:(hd + 1) * HEAD_DIM]
        qr = qh * cosf + pltpu.roll(qh, HEAD_DIM // 2, 1) * sinf
        q_ref[:, hd * HEAD_DIM:(hd + 1) * HEAD_DIM] = qr.astype(BF16)
    v_ref[...] = p[:, RET_WIDTH:2 * RET_WIDTH].astype(BF16)
    gate_ref[...] = p[:, 2 * RET_WIDTH:3 * RET_WIDTH].astype(BF16)
    pt = _dot_nt(wkut_ref[...], h)
    cost = cost_ref[...]
    sint = sint_ref[...]
    half = HEAD_DIM // 2
    for hd in range(RET_HEADS):
        kh = pt[hd * HEAD_DIM:(hd + 1) * HEAD_DIM, :]
        sw = jnp.concatenate([kh[half:, :], kh[:half, :]], axis=0)
        kt_ref[hd * HEAD_DIM:(hd + 1) * HEAD_DIM, :] = (kh * cost + sw * sint).astype(BF16)
    ut_ref[...] = pt[RET_WIDTH:, :].astype(BF16)


def _inproj(x2, g_pre, wqvg, wkut, cosf, sinf, cost, sint, seq_len, tm):
    n = x2.shape[0]
    nseq = seq_len // tm
    return pl.pallas_call(
        _inproj_kernel,
        grid=(n // tm,),
        in_specs=[
            pl.BlockSpec((tm, D_MODEL), lambda i: (i, 0)),
            pl.BlockSpec((1, D_MODEL), lambda i: (0, 0)),
            pl.BlockSpec((D_MODEL, 3 * RET_WIDTH), lambda i: (0, 0)),
            pl.BlockSpec((RET_WIDTH + SSM_WIDTH, D_MODEL), lambda i: (0, 0)),
            pl.BlockSpec((tm, HEAD_DIM), lambda i: (i % nseq, 0)),
            pl.BlockSpec((tm, HEAD_DIM), lambda i: (i % nseq, 0)),
            pl.BlockSpec((HEAD_DIM, tm), lambda i: (0, i % nseq)),
            pl.BlockSpec((HEAD_DIM, tm), lambda i: (0, i % nseq)),
        ],
        out_specs=[
            pl.BlockSpec((tm, RET_WIDTH), lambda i: (i, 0)),
            pl.BlockSpec((tm, RET_WIDTH), lambda i: (i, 0)),
            pl.BlockSpec((tm, RET_WIDTH), lambda i: (i, 0)),
            pl.BlockSpec((RET_WIDTH, tm), lambda i: (0, i)),
            pl.BlockSpec((SSM_WIDTH, tm), lambda i: (0, i)),
        ],
        out_shape=[
            jax.ShapeDtypeStruct((n, RET_WIDTH), BF16),
            jax.ShapeDtypeStruct((n, RET_WIDTH), BF16),
            jax.ShapeDtypeStruct((n, RET_WIDTH), BF16),
            jax.ShapeDtypeStruct((RET_WIDTH, n), BF16),
            jax.ShapeDtypeStruct((SSM_WIDTH, n), BF16),
        ],
        compiler_params=pltpu.CompilerParams(
            dimension_semantics=("parallel",), vmem_limit_bytes=VMEM_LIMIT_BYTES),
        name="inproj",
    )(x2, g_pre, wqvg, wkut, cosf, sinf, cost, sint)


def _retention_kernel(q_ref, kt_ref, v_ref, gate_ref, decay_ref, xi_ref, zeta_ref, gch_ref, gn_ref,
                      y_ref, r_ref):
    @pl.when(pl.program_id(1) == 0)
    def _():
        r_ref[...] = jnp.zeros_like(r_ref)

    for hd in range(RET_HEADS):
        sl = slice(hd * HEAD_DIM, (hd + 1) * HEAD_DIM)
        q = q_ref[:, sl]
        kt = kt_ref[sl, :]
        v = v_ref[:, sl]
        s = _dot(q, kt) * decay_ref[hd]
        inner = _dot(s.astype(BF16), v)
        r_prev = r_ref[hd]
        cross = _dot(q, r_prev.astype(BF16)) * xi_ref[hd]
        ktz = (kt.astype(F32) * zeta_ref[hd]).astype(BF16)
        r_ref[hd] = gch_ref[hd] * r_prev + _dot(ktz, v)
        y = inner + cross
        mu = jnp.mean(y, axis=-1, keepdims=True)
        yc = y - mu
        var = jnp.mean(yc * yc, axis=-1, keepdims=True)
        yn = yc * lax.rsqrt(var + NORM_EPS) * gn_ref[hd]
        y_ref[:, sl] = (jax.nn.silu(gate_ref[:, sl].astype(F32)) * yn).astype(BF16)


def _retention(q, kt, v, gate, decay, xi, zeta, gch, gn, batch, nc):
    n = q.shape[0]
    tok = lambda b, c: (b * nc + c, 0)
    const3 = lambda b, c: (0, 0, 0)
    return pl.pallas_call(
        _retention_kernel,
        grid=(batch, nc),
        in_specs=[
            pl.BlockSpec((CHUNK, RET_WIDTH), tok),
            pl.BlockSpec((RET_WIDTH, CHUNK), lambda b, c: (0, b * nc + c)),
            pl.BlockSpec((CHUNK, RET_WIDTH), tok),
            pl.BlockSpec((CHUNK, RET_WIDTH), tok),
            pl.BlockSpec((RET_HEADS, CHUNK, CHUNK), const3),
            pl.BlockSpec((RET_HEADS, CHUNK, HEAD_DIM), const3),
            pl.BlockSpec((RET_HEADS, 1, CHUNK), const3),
            pl.BlockSpec((RET_HEADS, 1, HEAD_DIM), const3),
            pl.BlockSpec((RET_HEADS, 1, HEAD_DIM), const3),
        ],
        out_specs=pl.BlockSpec((CHUNK, RET_WIDTH), tok),
        out_shape=jax.ShapeDtypeStruct((n, RET_WIDTH), BF16),
        scratch_shapes=[pltpu.VMEM((RET_HEADS, HEAD_DIM, HEAD_DIM), F32)],
        compiler_params=pltpu.CompilerParams(
            dimension_semantics=("parallel", "arbitrary"), vmem_limit_bytes=VMEM_LIMIT_BYTES),
        name="retention",
    )(q, kt, v, gate, decay, xi, zeta, gch, gn)


def _s5_kernel(u_ref, ktab_ref, wst_ref, wc_ref, lam_ref, dsk_ref, y_ref, ktoe_ref, *, nc, nlev):
    m = u_ref.shape[1]
    row = lax.broadcasted_iota(jnp.int32, (CHUNK, CHUNK), 0)
    col = lax.broadcasted_iota(jnp.int32, (CHUNK, CHUNK), 1)
    causal = col >= row

    def build(ci, carry):
        r0 = pl.multiple_of(ci * CHUNK, CHUNK)
        for co in range(SSM_GROUP):
            taps = ktab_ref[0, pl.ds(ci * SSM_GROUP + co, 1), :]
            tile = jnp.broadcast_to(taps, (CHUNK, CHUNK))
            tile = pltpu.roll(tile, 0, 1, stride=1, stride_axis=0)
            tile = jnp.where(causal, tile, 0.0)
            ktoe_ref[pl.ds(r0, CHUNK), co * CHUNK:(co + 1) * CHUNK] = tile.astype(BF16)
        return carry

    lax.fori_loop(0, SSM_GROUP, build, 0)

    a = jnp.concatenate([u_ref[ci] for ci in range(SSM_GROUP)], axis=1)
    y = _dot(a, ktoe_ref[...])
    st = _dot(a, wst_ref[0])

    chunk_id = lax.broadcasted_iota(jnp.int32, (m, LANES), 0) % nc
    for lev in range(nlev):
        k = 1 << lev
        sh = jnp.where(chunk_id >= k, pltpu.roll(st, k, 0), 0.0)
        st = st + sh * lam_ref[0, 2 * lev:2 * lev + 1, :] \
            + pltpu.roll(sh, SSM_STATE, 1) * lam_ref[0, 2 * lev + 1:2 * lev + 2, :]
    xprev = jnp.where(chunk_id >= 1, pltpu.roll(st, 1, 0), 0.0)
    y = y + _dot(xprev.astype(BF16), wc_ref[0])

    for co in range(SSM_GROUP):
        yc = y[:, co * CHUNK:(co + 1) * CHUNK] + dsk_ref[0, co:co + 1, :] * u_ref[co].astype(F32)
        y_ref[co] = jax.nn.gelu(yc).astype(BF16)


def _s5(ut3, ktab, wst, wc, lamtab, dsk, nc, nlev):
    m = ut3.shape[1]
    grp = lambda g: (g, 0, 0)
    return pl.pallas_call(
        functools.partial(_s5_kernel, nc=nc, nlev=nlev),
        grid=(SSM_GROUPS,),
        in_specs=[
            pl.BlockSpec((SSM_GROUP, m, CHUNK), grp),
            pl.BlockSpec((1, SSM_GROUP * SSM_GROUP, CHUNK), grp),
            pl.BlockSpec((1, SSM_GROUP * CHUNK, 2 * SSM_STATE), grp),
            pl.BlockSpec((1, 2 * SSM_STATE, SSM_GROUP * CHUNK), grp),
            pl.BlockSpec((1, lamtab.shape[1], LANES), grp),
            pl.BlockSpec((1, SSM_GROUP, LANES), grp),
        ],
        out_specs=pl.BlockSpec((SSM_GROUP, m, CHUNK), grp),
        out_shape=jax.ShapeDtypeStruct(ut3.shape, BF16),
        scratch_shapes=[pltpu.VMEM((SSM_GROUP * CHUNK, SSM_GROUP * CHUNK), BF16)],
        compiler_params=pltpu.CompilerParams(
            dimension_semantics=("parallel",), vmem_limit_bytes=VMEM_LIMIT_BYTES),
        name="s5",
    )(ut3, ktab, wst, wc, lamtab, dsk)


def _rms(x, g):
    ms = jnp.mean(x * x, axis=-1, keepdims=True)
    return x * lax.rsqrt(ms + NORM_EPS) * g


def _mlp_kernel(x_ref, yret_ref, yst_ref, wglu_ref, wout_ref, gpost_ref, gpre2_ref, gpost2_ref,
                w1_ref, w2_ref, o_ref):
    glu = _dot_tn(yst_ref[...], wglu_ref[...])
    yssm = (glu[:, :SSM_WIDTH] * jax.nn.sigmoid(glu[:, SSM_WIDTH:])).astype(BF16)
    mix = _dot(yret_ref[...], wout_ref[:RET_WIDTH, :]) + _dot(yssm, wout_ref[RET_WIDTH:, :])
    x1 = x_ref[...] + _rms(mix, gpost_ref[...])
    h = _rms(x1, gpre2_ref[...]).astype(BF16)
    f = jnp.maximum(_dot(h, w1_ref[...]), 0.0)
    mm = _dot((f * f).astype(BF16), w2_ref[...])
    o_ref[...] = x1 + _rms(mm, gpost2_ref[...])


def _mlp(x2, yret, yst, wglu, wout, gpost, gpre2, gpost2, w1, w2, tm):
    n = x2.shape[0]
    const = lambda i: (0, 0)
    once = pl.Buffered(1)
    return pl.pallas_call(
        _mlp_kernel,
        grid=(n // tm,),
        in_specs=[
            pl.BlockSpec((tm, D_MODEL), lambda i: (i, 0)),
            pl.BlockSpec((tm, RET_WIDTH), lambda i: (i, 0)),
            pl.BlockSpec((SSM_WIDTH, tm), lambda i: (0, i)),
            pl.BlockSpec((SSM_WIDTH, 2 * SSM_WIDTH), const, pipeline_mode=once),
            pl.BlockSpec((D_MODEL, D_MODEL), const, pipeline_mode=once),
            pl.BlockSpec((1, D_MODEL), const),
            pl.BlockSpec((1, D_MODEL), const),
            pl.BlockSpec((1, D_MODEL), const),
            pl.BlockSpec((D_MODEL, D_FF), const, pipeline_mode=once),
            pl.BlockSpec((D_FF, D_MODEL), const, pipeline_mode=once),
        ],
        out_specs=pl.BlockSpec((tm, D_MODEL), lambda i: (i, 0)),
        out_shape=jax.ShapeDtypeStruct((n, D_MODEL), F32),
        compiler_params=pltpu.CompilerParams(
            dimension_semantics=("parallel",), vmem_limit_bytes=VMEM_LIMIT_BYTES),
        name="mlp",
    )(x2, yret, yst, wglu, wout, gpost, gpre2, gpost2, w1, w2)


def _rope_tables(seq_len):
    half = HEAD_DIM // 2
    inv_freq = ROPE_BASE ** (-jnp.arange(half, dtype=F32) / half)
    ang = jnp.arange(seq_len, dtype=F32)[:, None] * inv_freq[None, :]
    cos, sin = jnp.cos(ang), jnp.sin(ang)
    cosf = jnp.concatenate([cos, cos], axis=1)
    sinf = jnp.concatenate([-sin, sin], axis=1)
    kscale = HEAD_DIM ** -0.5
    return cosf, sinf, cosf.T * kscale, sinf.T * kscale


def _retention_tables():
    log_gamma = jnp.log(1.0 - jnp.exp(jnp.linspace(math.log(1.0 / 32), math.log(1.0 / 512), RET_HEADS))).astype(F32)
    idx = jnp.arange(CHUNK, dtype=F32)
    diff = idx[:, None] - idx[None, :]
    decay = jnp.where(diff[None] >= 0, jnp.exp(jnp.maximum(diff, 0.0)[None] * log_gamma[:, None, None]), 0.0)
    zeta = jnp.exp((CHUNK - 1 - idx)[None, :] * log_gamma[:, None])
    xi = jnp.exp((idx + 1.0)[None, :] * log_gamma[:, None])
    gch = jnp.exp(CHUNK * log_gamma)
    xi_b = jnp.broadcast_to(xi[:, :, None], (RET_HEADS, CHUNK, HEAD_DIM))
    gch_b = jnp.broadcast_to(gch[:, None, None], (RET_HEADS, 1, HEAD_DIM))
    return decay, xi_b, zeta[:, None, :], gch_b


def _s5_tables(lam_re, lam_im, log_dt, b_re, b_im, c_re, c_im, d_skip, nlev):
    lam = lax.complex(jnp.minimum(lam_re.astype(F32), -1e-4), lam_im.astype(F32))
    dt = jnp.exp(log_dt.astype(F32))[:, None]
    ldt = lam * dt
    lam_bar = jnp.exp(ldt)
    b_c = lax.complex(b_re.astype(F32), b_im.astype(F32))
    b_bar = ((lam_bar - 1.0) / lam)[:, :, None] * b_c
    c_c = lax.complex(c_re.astype(F32), c_im.astype(F32))
    tau = jnp.arange(CHUNK + 1, dtype=F32)
    pw = jnp.exp(ldt[:, :, None] * tau[None, None, :])
    ktab = jnp.real(jnp.einsum('gop,gpt,gpi->giot', c_c, pw[:, :, :CHUNK], b_bar))
    ktab = ktab.reshape(SSM_GROUPS, SSM_GROUP * SSM_GROUP, CHUNK).astype(F32)
    wst = b_bar[:, :, :, None] * pw[:, :, None, CHUNK - 1::-1][..., :CHUNK]
    wst = jnp.transpose(wst, (0, 2, 3, 1))
    wst = jnp.concatenate([jnp.real(wst), jnp.imag(wst)], axis=-1)
    wst = wst.reshape(SSM_GROUPS, SSM_GROUP * CHUNK, 2 * SSM_STATE).astype(BF16)
    wc = c_c[:, :, :, None] * pw[:, None, :, 1:CHUNK + 1]
    wc = jnp.transpose(wc, (0, 2, 1, 3))
    wc = jnp.concatenate([jnp.real(wc), -jnp.imag(wc)], axis=1)
    wc = wc.reshape(SSM_GROUPS, 2 * SSM_STATE, SSM_GROUP * CHUNK).astype(BF16)
    rows = []
    for lev in range(nlev):
        pwr = jnp.exp(ldt * float(CHUNK * (1 << lev)))
        rows.append(jnp.concatenate([jnp.real(pwr), jnp.real(pwr)], axis=-1))
        rows.append(jnp.concatenate([-jnp.imag(pwr), jnp.imag(pwr)], axis=-1))
    nrows = max(8, -(-2 * nlev // 8) * 8)
    while len(rows) < nrows:
        rows.append(jnp.zeros((SSM_GROUPS, 2 * SSM_STATE), F32))
    lamtab = jnp.stack(rows, axis=1).astype(F32)
    dsk = jnp.broadcast_to(d_skip.astype(F32).reshape(SSM_GROUPS, SSM_GROUP, 1), (SSM_GROUPS, SSM_GROUP, LANES))
    return ktab, wst, wc, lamtab, dsk


def kernel(x, norm_mix_pre, norm_mix_post, w_in, ret_gn_gain, ssm_lambda_re, ssm_lambda_im, ssm_log_dt,
           ssm_b_re, ssm_b_im, ssm_c_re, ssm_c_im, ssm_d, w_glu, w_out, norm_mlp_pre, norm_mlp_post,
           w_ff1, w_ff2):
    batch, seq_len, _ = x.shape
    depth = w_in.shape[0]
    n = batch * seq_len
    nc = seq_len // CHUNK
    nlev = max(1, (nc - 1).bit_length())
    tm = min(512, seq_len)
    assert seq_len % CHUNK == 0 and seq_len % tm == 0

    cosf, sinf, cost, sint = _rope_tables(seq_len)
    decay, xi_b, zeta, gch_b = _retention_tables()
    x2 = x.reshape(n, D_MODEL)
    for i in range(depth):
        wi = w_in[i]
        wqvg = jnp.concatenate([wi[:, :RET_WIDTH], wi[:, 2 * RET_WIDTH:4 * RET_WIDTH]], axis=1).astype(BF16)
        wkut = jnp.concatenate([wi[:, RET_WIDTH:2 * RET_WIDTH], wi[:, 4 * RET_WIDTH:]], axis=1).T.astype(BF16)
        q, v, gate, kt, ut = _inproj(x2, norm_mix_pre[i][None, :], wqvg, wkut, cosf, sinf, cost, sint,
                                     seq_len, tm)
        gn = ret_gn_gain[i].astype(F32).reshape(RET_HEADS, 1, HEAD_DIM)
        yret = _retention(q, kt, v, gate, decay, xi_b, zeta, gch_b, gn, batch, nc)
        ktab, wst, wc, lamtab, dsk = _s5_tables(
            ssm_lambda_re[i], ssm_lambda_im[i], ssm_log_dt[i], ssm_b_re[i], ssm_b_im[i],
            ssm_c_re[i], ssm_c_im[i], ssm_d[i], nlev)
        yst3 = _s5(ut.reshape(SSM_WIDTH, n // CHUNK, CHUNK), ktab, wst, wc, lamtab, dsk, nc, nlev)
        yst = yst3.reshape(SSM_WIDTH, n)
        x2 = _mlp(x2, yret, yst, w_glu[i].astype(BF16), w_out[i].astype(BF16),
                  norm_mix_post[i][None, :], norm_mlp_pre[i][None, :], norm_mlp_post[i][None, :],
                  w_ff1[i].astype(BF16), w_ff2[i].astype(BF16), tm)
    return x2.reshape(batch, seq_len, D_MODEL)
```

```python
import functools
import math

import jax
import jax.numpy as jnp
from jax import lax
from jax.experimental import pallas as pl
from jax.experimental.pallas import tpu as pltpu

D_MODEL = 1024
RET_WIDTH = 512
RET_HEADS = 4
HEAD_DIM = 128
CHUNK = 128
ROPE_BASE = 10000.0
SSM_WIDTH = 512
SSM_GROUP = 16
SSM_GROUPS = 32
SSM_STATE = 64
D_FF = 4096
NORM_EPS = 1e-6
LANES = 128
MXU_DIM = 256

BF16 = jnp.bfloat16
F32 = jnp.float32

VMEM_LIMIT_BYTES = 56 * 1024 * 1024

ROW_BRR, ROW_BII, ROW_BB1, ROW_BB2, ROW_RC1, ROW_RC2, ROW_LAM, ROW_DSK = 0, 16, 32, 48, 64, 80, 96, 112
MAX_SCAN_LEVELS = (ROW_DSK - ROW_LAM) // 2


def _dot(a, b):
    return jnp.dot(a, b, preferred_element_type=F32)


def _dot_nt(a, b):
    return lax.dot_general(a, b, (((1,), (1,)), ((), ())), preferred_element_type=F32)


def _dot_tn(a, b):
    return lax.dot_general(a, b, (((0,), (0,)), ((), ())), preferred_element_type=F32)


def _rms(x, g):
    ms = jnp.mean(x * x, axis=-1, keepdims=True)
    return x * lax.rsqrt(ms + NORM_EPS) * g


def _mix_in_kernel(x_ref, g_ref, wqvg_ref, wkut_ref, cosf_ref, sinf_ref, cost_ref, sint_ref,
                   xiq_ref, ztt_ref, gch_ref, gn_ref, yret_ref, ut_ref, r_ref):
    @pl.when(pl.program_id(1) == 0)
    def _():
        r_ref[...] = jnp.zeros_like(r_ref)

    tm = x_ref.shape[0]
    half = HEAD_DIM // 2
    h = _rms(x_ref[...], g_ref[...]).astype(BF16)
    p = _dot(h, wqvg_ref[...])
    pt = _dot_nt(wkut_ref[...], h)
    ut_ref[...] = pt[RET_WIDTH:, :].astype(BF16)

    row = lax.broadcasted_iota(jnp.int32, (CHUNK, CHUNK), 0)
    col = lax.broadcasted_iota(jnp.int32, (CHUNK, CHUNK), 1)
    causal = row >= col
    cosf, sinf = cosf_ref[...], sinf_ref[...]
    cost, sint = cost_ref[...], sint_ref[...]
    for hd in range(RET_HEADS):
        sl = slice(hd * HEAD_DIM, (hd + 1) * HEAD_DIM)
        qh = p[:, sl]
        qh = qh * cosf + pltpu.roll(qh, half, 1) * sinf
        kh = pt[sl, :]
        sw = jnp.concatenate([kh[half:, :], kh[:half, :]], axis=0)
        kh = ((kh * cost + sw * sint) * ztt_ref[hd]).astype(BF16)
        g_state = r_ref[hd]
        for c in range(tm // CHUNK):
            rows = slice(c * CHUNK, (c + 1) * CHUNK)
            qc = (qh[rows, :] * xiq_ref[hd]).astype(BF16)
            kc = kh[:, rows]
            vc = p[rows, RET_WIDTH + hd * HEAD_DIM:RET_WIDTH + (hd + 1) * HEAD_DIM].astype(BF16)
            s = jnp.where(causal, _dot(qc, kc), 0.0).astype(BF16)
            y = _dot(s, vc) + _dot(qc, g_state.astype(BF16))
            g_state = gch_ref[hd] * (g_state + _dot(kc, vc))
            mu = jnp.mean(y, axis=-1, keepdims=True)
            yc = y - mu
            var = jnp.mean(yc * yc, axis=-1, keepdims=True)
            yn = yc * lax.rsqrt(var + NORM_EPS) * gn_ref[hd]
            gate = p[rows, 2 * RET_WIDTH + hd * HEAD_DIM:2 * RET_WIDTH + (hd + 1) * HEAD_DIM]
            yret_ref[rows, sl] = (jax.nn.silu(gate) * yn).astype(BF16)
        r_ref[hd] = g_state


def _mix_in(x2, g_pre, wqvg, wkut, cosf, sinf, cost, sint, xiq, ztt, gch, gn, batch, seq_len, tm):
    n = x2.shape[0]
    nseq = seq_len // tm
    tok = lambda b, j: (b * nseq + j, 0)
    tokt = lambda b, j: (0, b * nseq + j)
    const2 = lambda b, j: (0, 0)
    const3 = lambda b, j: (0, 0, 0)
    return pl.pallas_call(
        _mix_in_kernel,
        grid=(batch, nseq),
        in_specs=[
            pl.BlockSpec((tm, D_MODEL), tok),
            pl.BlockSpec((1, D_MODEL), const2),
            pl.BlockSpec((D_MODEL, 3 * RET_WIDTH), const2),
            pl.BlockSpec((RET_WIDTH + SSM_WIDTH, D_MODEL), const2),
            pl.BlockSpec((tm, HEAD_DIM), lambda b, j: (j, 0)),
            pl.BlockSpec((tm, HEAD_DIM), lambda b, j: (j, 0)),
            pl.BlockSpec((HEAD_DIM, tm), lambda b, j: (0, j)),
            pl.BlockSpec((HEAD_DIM, tm), lambda b, j: (0, j)),
            pl.BlockSpec((RET_HEADS, CHUNK, HEAD_DIM), const3),
            pl.BlockSpec((RET_HEADS, 1, tm), const3),
            pl.BlockSpec((RET_HEADS, 1, HEAD_DIM), const3),
            pl.BlockSpec((RET_HEADS, 1, HEAD_DIM), const3),
        ],
        out_specs=[
            pl.BlockSpec((tm, RET_WIDTH), tok),
            pl.BlockSpec((SSM_WIDTH, tm), tokt),
        ],
        out_shape=[
            jax.ShapeDtypeStruct((n, RET_WIDTH), BF16),
            jax.ShapeDtypeStruct((SSM_WIDTH, n), BF16),
        ],
        scratch_shapes=[pltpu.VMEM((RET_HEADS, HEAD_DIM, HEAD_DIM), F32)],
        compiler_params=pltpu.CompilerParams(
            dimension_semantics=("parallel", "arbitrary"), vmem_limit_bytes=VMEM_LIMIT_BYTES),
        name="mix_in",
    )(x2, g_pre, wqvg, wkut, cosf, sinf, cost, sint, xiq, ztt, gch, gn)


def _s5_kernel(u_ref, pwk_ref, pst_ref, sm_ref, y_ref, wtap_ref, wst_ref, wct_ref, ktoe_ref, *, nc, nlev):
    m = u_ref.shape[1]
    nco = MXU_DIM // CHUNK

    def smrow(base, i):
        return sm_ref[0, base + i:base + i + 1, :]

    bb1 = sm_ref[0, ROW_BB1:ROW_BB1 + SSM_GROUP, :]
    bb2 = sm_ref[0, ROW_BB2:ROW_BB2 + SSM_GROUP, :]
    mcat = jnp.concatenate(
        [bb1 * smrow(ROW_RC1, co) + bb2 * smrow(ROW_RC2, co) for co in range(SSM_GROUP)], axis=0)
    taps = jnp.dot(mcat, pwk_ref[0], precision=lax.Precision.HIGHEST, preferred_element_type=F32)
    wtap_ref[...] = taps

    a1, a2 = pst_ref[0, 0], pst_ref[0, 1]
    for ci in range(SSM_GROUP):
        wst_ref[ci * CHUNK:(ci + 1) * CHUNK, :] = (
            a1 * smrow(ROW_BRR, ci) + a2 * smrow(ROW_BII, ci)).astype(BF16)
    b1, b2 = pst_ref[0, 2], pst_ref[0, 3]
    for co in range(SSM_GROUP):
        wct_ref[co * CHUNK:(co + 1) * CHUNK, :] = (
            b1 * smrow(ROW_RC1, co) + b2 * smrow(ROW_RC2, co)).astype(BF16)

    a = jnp.concatenate([u_ref[ci] for ci in range(SSM_GROUP)], axis=1)
    st = _dot(a, wst_ref[...])

    chunk_id = lax.broadcasted_iota(jnp.int32, (m, LANES), 0) % nc
    for lev in range(nlev):
        k = 1 << lev
        sh = jnp.where(chunk_id >= k, pltpu.roll(st, k, 0), 0.0)
        st = st + sh * smrow(ROW_LAM, 2 * lev) + pltpu.roll(sh, SSM_STATE, 1) * smrow(ROW_LAM, 2 * lev + 1)
    xprev = jnp.where(chunk_id >= 1, pltpu.roll(st, 1, 0), 0.0).astype(BF16)

    trow = lax.broadcasted_iota(jnp.int32, (CHUNK, CHUNK), 0)
    tcol = lax.broadcasted_iota(jnp.int32, (CHUNK, CHUNK), 1)
    valid = tcol >= trow
    for j in range(SSM_GROUP // nco):
        for cc in range(nco):
            co = j * nco + cc
            for ci in range(SSM_GROUP):
                tile = jnp.broadcast_to(wtap_ref[co * SSM_GROUP + ci:co * SSM_GROUP + ci + 1, :],
                                        (CHUNK, CHUNK))
                tile = pltpu.roll(tile, 0, 1, stride=1, stride_axis=0)
                ktoe_ref[j, ci * CHUNK:(ci + 1) * CHUNK, cc * CHUNK:(cc + 1) * CHUNK] = (
                    jnp.where(valid, tile, 0.0).astype(BF16))
        yj = _dot(a, ktoe_ref[j]) + _dot_nt(xprev, wct_ref[j * MXU_DIM:(j + 1) * MXU_DIM, :])
        for cc in range(nco):
            co = j * nco + cc
            yc = yj[:, cc * CHUNK:(cc + 1) * CHUNK] + smrow(ROW_DSK, co) * u_ref[co].astype(F32)
            y_ref[co] = jax.nn.gelu(yc).astype(BF16)


def _s5(ut3, pwk, pst, sm, nc, nlev):
    m = ut3.shape[1]
    grp = lambda g: (g, 0, 0)
    return pl.pallas_call(
        functools.partial(_s5_kernel, nc=nc, nlev=nlev),
        grid=(SSM_GROUPS,),
        in_specs=[
            pl.BlockSpec((SSM_GROUP, m, CHUNK), grp),
            pl.BlockSpec((1, 2 * SSM_STATE, CHUNK), grp),
            pl.BlockSpec((1, 4, CHUNK, 2 * SSM_STATE), lambda g: (g, 0, 0, 0)),
            pl.BlockSpec((1, LANES, LANES), grp),
        ],
        out_specs=pl.BlockSpec((SSM_GROUP, m, CHUNK), grp),
        out_shape=jax.ShapeDtypeStruct(ut3.shape, BF16),
        scratch_shapes=[
            pltpu.VMEM((SSM_GROUP * SSM_GROUP, CHUNK), F32),
            pltpu.VMEM((SSM_GROUP * CHUNK, 2 * SSM_STATE), BF16),
            pltpu.VMEM((SSM_GROUP * CHUNK, 2 * SSM_STATE), BF16),
            pltpu.VMEM((SSM_GROUP * CHUNK // MXU_DIM, SSM_GROUP * CHUNK, MXU_DIM), BF16),
        ],
        compiler_params=pltpu.CompilerParams(
            dimension_semantics=("parallel",), vmem_limit_bytes=VMEM_LIMIT_BYTES),
        name="s5",
    )(ut3, pwk, pst, sm)


def _mlp_kernel(x_ref, yret_ref, yst_ref, wglu_ref, wout_ref, gpost_ref, gpre2_ref, gpost2_ref,
                w1_ref, w2_ref, o_ref, *, ff_chunk):
    glu = _dot_tn(yst_ref[...], wglu_ref[...])
    yssm = (glu[:, :SSM_WIDTH] * jax.nn.sigmoid(glu[:, SSM_WIDTH:])).astype(BF16)
    mix = _dot(yret_ref[...], wout_ref[:RET_WIDTH, :]) + _dot(yssm, wout_ref[RET_WIDTH:, :])
    x1 = x_ref[...] + _rms(mix, gpost_ref[...])
    h = _rms(x1, gpre2_ref[...]).astype(BF16)
    mm = None
    for c in range(D_FF // ff_chunk):
        f = jnp.maximum(_dot(h, w1_ref[:, c * ff_chunk:(c + 1) * ff_chunk]), 0.0)
        part = _dot((f * f).astype(BF16), w2_ref[c * ff_chunk:(c + 1) * ff_chunk, :])
        mm = part if mm is None else mm + part
    o_ref[...] = x1 + _rms(mm, gpost2_ref[...])


def _mlp(x2, yret, yst, wglu, wout, gpost, gpre2, gpost2, w1, w2, tm, ff_chunk):
    n = x2.shape[0]
    const = lambda i: (0, 0)
    once = pl.Buffered(1)
    return pl.pallas_call(
        functools.partial(_mlp_kernel, ff_chunk=ff_chunk),
        grid=(n // tm,),
        in_specs=[
            pl.BlockSpec((tm, D_MODEL), lambda i: (i, 0)),
            pl.BlockSpec((tm, RET_WIDTH), lambda i: (i, 0)),
            pl.BlockSpec((SSM_WIDTH, tm), lambda i: (0, i)),
            pl.BlockSpec((SSM_WIDTH, 2 * SSM_WIDTH), const, pipeline_mode=once),
            pl.BlockSpec((D_MODEL, D_MODEL), const, pipeline_mode=once),
            pl.BlockSpec((1, D_MODEL), const),
            pl.BlockSpec((1, D_MODEL), const),
            pl.BlockSpec((1, D_MODEL), const),
            pl.BlockSpec((D_MODEL, D_FF), const, pipeline_mode=once),
            pl.BlockSpec((D_FF, D_MODEL), const, pipeline_mode=once),
        ],
        out_specs=pl.BlockSpec((tm, D_MODEL), lambda i: (i, 0)),
        out_shape=jax.ShapeDtypeStruct((n, D_MODEL), F32),
        compiler_params=pltpu.CompilerParams(
            dimension_semantics=("parallel",), vmem_limit_bytes=VMEM_LIMIT_BYTES),
        name="mlp",
    )(x2, yret, yst, wglu, wout, gpost, gpre2, gpost2, w1, w2)


def _rope_tables(seq_len):
    half = HEAD_DIM // 2
    inv_freq = ROPE_BASE ** (-jnp.arange(half, dtype=F32) / half)
    ang = jnp.arange(seq_len, dtype=F32)[:, None] * inv_freq[None, :]
    cos, sin = jnp.cos(ang), jnp.sin(ang)
    cosf = jnp.concatenate([cos, cos], axis=1)
    sinf = jnp.concatenate([-sin, sin], axis=1)
    kscale = HEAD_DIM ** -0.5
    return cosf, sinf, cosf.T * kscale, sinf.T * kscale


def _retention_tables(tm):
    log_gamma = jnp.log(1.0 - jnp.exp(jnp.linspace(math.log(1.0 / 32), math.log(1.0 / 512), RET_HEADS))).astype(F32)
    idx = jnp.arange(CHUNK, dtype=F32)
    xiq = jnp.exp((idx + 1.0 - CHUNK)[None, :] * log_gamma[:, None])
    zeta = jnp.exp((CHUNK - 1 - idx)[None, :] * log_gamma[:, None])
    gch = jnp.exp(CHUNK * log_gamma)
    xiq_b = jnp.broadcast_to(xiq[:, :, None], (RET_HEADS, CHUNK, HEAD_DIM))
    ztt = jnp.tile(zeta, (1, tm // CHUNK))[:, None, :]
    gch_b = jnp.broadcast_to(gch[:, None, None], (RET_HEADS, 1, HEAD_DIM))
    return xiq_b, ztt, gch_b


def _s5_tables(lam_re, lam_im, log_dt, b_re, b_im, c_re, c_im, d_skip, nlev):
    assert nlev <= MAX_SCAN_LEVELS
    a = jnp.minimum(lam_re.astype(F32), -1e-4)
    b = lam_im.astype(F32)
    dt = jnp.exp(log_dt.astype(F32))[:, None]
    tau = jnp.arange(CHUNK + 1, dtype=F32)
    mag = jnp.exp((a * dt)[:, :, None] * tau)
    ph = (b * dt)[:, :, None] * tau
    pr, pi = mag * jnp.cos(ph), mag * jnp.sin(ph)
    lr1, li = pr[:, :, 1] - 1.0, pi[:, :, 1]
    den = a * a + b * b
    c0r, c0i = (lr1 * a + li * b) / den, (li * a - lr1 * b) / den
    bre, bim = b_re.astype(F32), b_im.astype(F32)
    br = jnp.swapaxes(c0r[:, :, None] * bre - c0i[:, :, None] * bim, 1, 2)
    bi = jnp.swapaxes(c0r[:, :, None] * bim + c0i[:, :, None] * bre, 1, 2)
    cr, ci = c_re.astype(F32), c_im.astype(F32)

    pwk = jnp.concatenate([pr[:, :, :CHUNK], pi[:, :, :CHUNK]], axis=1)
    prr = jnp.swapaxes(pr[:, :, CHUNK - 1::-1], 1, 2)
    pir = jnp.swapaxes(pi[:, :, CHUNK - 1::-1], 1, 2)
    pr1 = jnp.swapaxes(pr[:, :, 1:], 1, 2)
    pi1 = jnp.swapaxes(pi[:, :, 1:], 1, 2)
    cat = lambda x, y: jnp.concatenate([x, y], axis=-1)
    pst = jnp.stack([cat(prr, pir), cat(-pir, prr), cat(pr1, -pi1), cat(-pi1, -pr1)], axis=1)

    lam_rows = []
    for lev in range(MAX_SCAN_LEVELS):
        step = float(CHUNK * (1 << lev))
        mg = jnp.exp(a * dt * step)
        re, im = mg * jnp.cos(b * dt * step), mg * jnp.sin(b * dt * step)
        lam_rows += [cat(re, re), cat(-im, im)]
    lamtab = jnp.stack(lam_rows, axis=1)
    dsk = jnp.broadcast_to(d_skip.astype(F32).reshape(SSM_GROUPS, SSM_GROUP, 1), (SSM_GROUPS, SSM_GROUP, LANES))
    sm = jnp.concatenate([cat(br, br), cat(bi, bi), cat(br, -bi), cat(-bi, -br), cat(cr, cr), cat(ci, ci),
                          lamtab, dsk], axis=1)
    return pwk, pst, sm


def _tiles(seq_len):
    tm_mix = min(1024, seq_len)
    tm_mlp = min(1024, seq_len)
    ff_chunk = 1024
    assert seq_len % CHUNK == 0 and seq_len % tm_mix == 0 and seq_len % tm_mlp == 0
    return tm_mix, tm_mlp, ff_chunk


def kernel(x, norm_mix_pre, norm_mix_post, w_in, ret_gn_gain, ssm_lambda_re, ssm_lambda_im, ssm_log_dt,
           ssm_b_re, ssm_b_im, ssm_c_re, ssm_c_im, ssm_d, w_glu, w_out, norm_mlp_pre, norm_mlp_post,
           w_ff1, w_ff2):
    batch, seq_len, _ = x.shape
    depth = w_in.shape[0]
    n = batch * seq_len
    nc = seq_len // CHUNK
    nlev = max(1, (nc - 1).bit_length())
    tm_mix, tm_mlp, ff_chunk = _tiles(seq_len)

    cosf, sinf, cost, sint = _rope_tables(seq_len)
    xiq, ztt, gch = _retention_tables(tm_mix)
    x2 = x.reshape(n, D_MODEL)
    for i in range(depth):
        wi = w_in[i]
        wqvg = jnp.concatenate([wi[:, :RET_WIDTH], wi[:, 2 * RET_WIDTH:4 * RET_WIDTH]], axis=1).astype(BF16)
        wkut = jnp.concatenate([wi[:, RET_WIDTH:2 * RET_WIDTH], wi[:, 4 * RET_WIDTH:]], axis=1).T.astype(BF16)
        gn = ret_gn_gain[i].astype(F32).reshape(RET_HEADS, 1, HEAD_DIM)
        yret, ut = _mix_in(x2, norm_mix_pre[i][None, :], wqvg, wkut, cosf, sinf, cost, sint,
                           xiq, ztt, gch, gn, batch, seq_len, tm_mix)
        pwk, pst, sm = _s5_tables(
            ssm_lambda_re[i], ssm_lambda_im[i], ssm_log_dt[i], ssm_b_re[i], ssm_b_im[i],
            ssm_c_re[i], ssm_c_im[i], ssm_d[i], nlev)
        yst3 = _s5(ut.reshape(SSM_WIDTH, n // CHUNK, CHUNK), pwk, pst, sm, nc, nlev)
        yst = yst3.reshape(SSM_WIDTH, n)
        x2 = _mlp(x2, yret, yst, w_glu[i].astype(BF16), w_out[i].astype(BF16),
                  norm_mix_post[i][None, :], norm_mlp_pre[i][None, :], norm_mlp_post[i][None, :],
                  w_ff1[i].astype(BF16), w_ff2[i].astype(BF16), tm_mlp, ff_chunk)
    return x2.reshape(batch, seq_len, D_MODEL)
```

```python
import functools
import math

import jax
import jax.numpy as jnp
from jax import lax
from jax.experimental import pallas as pl
from jax.experimental.pallas import tpu as pltpu

D_MODEL = 1024
RET_WIDTH = 512
RET_HEADS = 4
HEAD_DIM = 128
CHUNK = 128
ROPE_BASE = 10000.0
SSM_WIDTH = 512
SSM_GROUP = 16
SSM_GROUPS = 32
SSM_STATE = 64
D_FF = 4096
NORM_EPS = 1e-6
LANES = 128
MXU_DIM = 256

BF16 = jnp.bfloat16
F32 = jnp.float32

VMEM_LIMIT_BYTES = 56 * 1024 * 1024

ROW_BRR, ROW_BII, ROW_BB1, ROW_BB2, ROW_RC1, ROW_RC2, ROW_LAM, ROW_DSK = 0, 16, 32, 48, 64, 80, 96, 112
MAX_SCAN_LEVELS = (ROW_DSK - ROW_LAM) // 2


def _dot(a, b):
    return jnp.dot(a, b, preferred_element_type=F32)


def _dot_nt(a, b):
    return lax.dot_general(a, b, (((1,), (1,)), ((), ())), preferred_element_type=F32)


def _dot_tn(a, b):
    return lax.dot_general(a, b, (((0,), (0,)), ((), ())), preferred_element_type=F32)


def _rms(x, g):
    ms = jnp.mean(x * x, axis=-1, keepdims=True)
    return x * lax.rsqrt(ms + NORM_EPS) * g


def _mix_in_kernel(x_ref, g_ref, wqvg_ref, wkut_ref, cosf_ref, sinf_ref, cost_ref, sint_ref,
                   xiq_ref, ztt_ref, gch_ref, gn_ref, yret_ref, ut_ref, r_ref):
    @pl.when(pl.program_id(1) == 0)
    def _():
        r_ref[...] = jnp.zeros_like(r_ref)

    tm = x_ref.shape[0]
    half = HEAD_DIM // 2
    h = _rms(x_ref[...], g_ref[...]).astype(BF16)
    p = _dot(h, wqvg_ref[...])
    pt = _dot_nt(wkut_ref[...], h)
    ncs = tm // CHUNK
    for c in range(ncs):
        ut_ref[0, pl.ds(c, SSM_WIDTH, stride=ncs), :] = pt[RET_WIDTH:, c * CHUNK:(c + 1) * CHUNK]

    row = lax.broadcasted_iota(jnp.int32, (CHUNK, CHUNK), 0)
    col = lax.broadcasted_iota(jnp.int32, (CHUNK, CHUNK), 1)
    causal = row >= col
    cosf, sinf = cosf_ref[...], sinf_ref[...]
    cost, sint = cost_ref[...], sint_ref[...]
    for hd in range(RET_HEADS):
        sl = slice(hd * HEAD_DIM, (hd + 1) * HEAD_DIM)
        qh = p[:, sl]
        qh = qh * cosf + pltpu.roll(qh, half, 1) * sinf
        kh = pt[sl, :]
        sw = jnp.concatenate([kh[half:, :], kh[:half, :]], axis=0)
        kh = ((kh * cost + sw * sint) * ztt_ref[hd]).astype(BF16)
        g_state = r_ref[hd]
        for c in range(tm // CHUNK):
            rows = slice(c * CHUNK, (c + 1) * CHUNK)
            qc = (qh[rows, :] * xiq_ref[hd]).astype(BF16)
            kc = kh[:, rows]
            vc = p[rows, RET_WIDTH + hd * HEAD_DIM:RET_WIDTH + (hd + 1) * HEAD_DIM].astype(BF16)
            s = jnp.where(causal, _dot(qc, kc), 0.0).astype(BF16)
            y = _dot(s, vc) + _dot(qc, g_state.astype(BF16))
            g_state = gch_ref[hd] * (g_state + _dot(kc, vc))
            mu = jnp.mean(y, axis=-1, keepdims=True)
            yc = y - mu
            var = jnp.mean(yc * yc, axis=-1, keepdims=True)
            yn = yc * lax.rsqrt(var + NORM_EPS) * gn_ref[hd]
            gate = p[rows, 2 * RET_WIDTH + hd * HEAD_DIM:2 * RET_WIDTH + (hd + 1) * HEAD_DIM]
            yret_ref[rows, sl] = (jax.nn.silu(gate) * yn).astype(BF16)
        r_ref[hd] = g_state


def _mix_in(x2, g_pre, wqvg, wkut, cosf, sinf, cost, sint, xiq, ztt, gch, gn, batch, seq_len, tm):
    n = x2.shape[0]
    nseq = seq_len // tm
    tok = lambda b, j: (b * nseq + j, 0)
    tile3 = lambda b, j: (b * nseq + j, 0, 0)
    const2 = lambda b, j: (0, 0)
    const3 = lambda b, j: (0, 0, 0)
    return pl.pallas_call(
        _mix_in_kernel,
        grid=(batch, nseq),
        in_specs=[
            pl.BlockSpec((tm, D_MODEL), tok),
            pl.BlockSpec((1, D_MODEL), const2),
            pl.BlockSpec((D_MODEL, 3 * RET_WIDTH), const2),
            pl.BlockSpec((RET_WIDTH + SSM_WIDTH, D_MODEL), const2),
            pl.BlockSpec((tm, HEAD_DIM), lambda b, j: (j, 0)),
            pl.BlockSpec((tm, HEAD_DIM), lambda b, j: (j, 0)),
            pl.BlockSpec((HEAD_DIM, tm), lambda b, j: (0, j)),
            pl.BlockSpec((HEAD_DIM, tm), lambda b, j: (0, j)),
            pl.BlockSpec((RET_HEADS, CHUNK, HEAD_DIM), const3),
            pl.BlockSpec((RET_HEADS, 1, tm), const3),
            pl.BlockSpec((RET_HEADS, 1, HEAD_DIM), const3),
            pl.BlockSpec((RET_HEADS, 1, HEAD_DIM), const3),
        ],
        out_specs=[
            pl.BlockSpec((tm, RET_WIDTH), tok),
            pl.BlockSpec((1, SSM_WIDTH * (tm // CHUNK), CHUNK), tile3),
        ],
        out_shape=[
            jax.ShapeDtypeStruct((n, RET_WIDTH), BF16),
            jax.ShapeDtypeStruct((n // tm, SSM_WIDTH * (tm // CHUNK), CHUNK), F32),
        ],
        scratch_shapes=[pltpu.VMEM((RET_HEADS, HEAD_DIM, HEAD_DIM), F32)],
        compiler_params=pltpu.CompilerParams(
            dimension_semantics=("parallel", "arbitrary"), vmem_limit_bytes=VMEM_LIMIT_BYTES),
        name="mix_in",
    )(x2, g_pre, wqvg, wkut, cosf, sinf, cost, sint, xiq, ztt, gch, gn)


def _s5_kernel(u_ref, pwk_ref, pst_ref, sm_ref, y_ref, wtap_ref, wst_ref, wct_ref, ktoe_ref, *, nc, nlev):
    m = u_ref.shape[0] * u_ref.shape[2]
    nco = MXU_DIM // CHUNK

    def smrow(base, i):
        return sm_ref[0, base + i:base + i + 1, :]

    bb1 = sm_ref[0, ROW_BB1:ROW_BB1 + SSM_GROUP, :]
    bb2 = sm_ref[0, ROW_BB2:ROW_BB2 + SSM_GROUP, :]
    mcat = jnp.concatenate(
        [bb1 * smrow(ROW_RC1, co) + bb2 * smrow(ROW_RC2, co) for co in range(SSM_GROUP)], axis=0)
    taps = jnp.dot(mcat, pwk_ref[0], precision=lax.Precision.HIGHEST, preferred_element_type=F32)
    wtap_ref[...] = taps

    a1, a2 = pst_ref[0, 0], pst_ref[0, 1]
    for ci in range(SSM_GROUP):
        wst_ref[ci * CHUNK:(ci + 1) * CHUNK, :] = (
            a1 * smrow(ROW_BRR, ci) + a2 * smrow(ROW_BII, ci)).astype(BF16)
    b1, b2 = pst_ref[0, 2], pst_ref[0, 3]
    for co in range(SSM_GROUP):
        wct_ref[co * CHUNK:(co + 1) * CHUNK, :] = (
            b1 * smrow(ROW_RC1, co) + b2 * smrow(ROW_RC2, co)).astype(BF16)

    u = [u_ref[:, ci].reshape(m, CHUNK) for ci in range(SSM_GROUP)]
    a = jnp.concatenate([uc.astype(BF16) for uc in u], axis=1)
    st = _dot(a, wst_ref[...])

    chunk_id = lax.broadcasted_iota(jnp.int32, (m, LANES), 0) % nc
    for lev in range(nlev):
        k = 1 << lev
        sh = jnp.where(chunk_id >= k, pltpu.roll(st, k, 0), 0.0)
        st = st + sh * smrow(ROW_LAM, 2 * lev) + pltpu.roll(sh, SSM_STATE, 1) * smrow(ROW_LAM, 2 * lev + 1)
    xprev = jnp.where(chunk_id >= 1, pltpu.roll(st, 1, 0), 0.0).astype(BF16)

    trow = lax.broadcasted_iota(jnp.int32, (CHUNK, CHUNK), 0)
    tcol = lax.broadcasted_iota(jnp.int32, (CHUNK, CHUNK), 1)
    valid = tcol >= trow
    for j in range(SSM_GROUP // nco):
        for cc in range(nco):
            co = j * nco + cc
            for ci in range(SSM_GROUP):
                tile = jnp.broadcast_to(wtap_ref[co * SSM_GROUP + ci:co * SSM_GROUP + ci + 1, :],
                                        (CHUNK, CHUNK))
                tile = pltpu.roll(tile, 0, 1, stride=1, stride_axis=0)
                ktoe_ref[j, ci * CHUNK:(ci + 1) * CHUNK, cc * CHUNK:(cc + 1) * CHUNK] = (
                    jnp.where(valid, tile, 0.0).astype(BF16))
        yj = _dot(a, ktoe_ref[j]) + _dot_nt(xprev, wct_ref[j * MXU_DIM:(j + 1) * MXU_DIM, :])
        for cc in range(nco):
            co = j * nco + cc
            yc = yj[:, cc * CHUNK:(cc + 1) * CHUNK] + smrow(ROW_DSK, co) * u[co]
            y_ref[:, co] = jax.nn.gelu(yc).reshape(y_ref.shape[0], y_ref.shape[2], CHUNK)


def _s5(ut4, pwk, pst, sm, nc, nlev):
    nt, _, ncs, _ = ut4.shape
    grp = lambda g: (g, 0, 0)
    ugrp = lambda g: (0, g, 0, 0)
    return pl.pallas_call(
        functools.partial(_s5_kernel, nc=nc, nlev=nlev),
        grid=(SSM_GROUPS,),
        in_specs=[
            pl.BlockSpec((nt, SSM_GROUP, ncs, CHUNK), ugrp),
            pl.BlockSpec((1, 2 * SSM_STATE, CHUNK), grp),
            pl.BlockSpec((1, 4, CHUNK, 2 * SSM_STATE), lambda g: (g, 0, 0, 0)),
            pl.BlockSpec((1, LANES, LANES), grp),
        ],
        out_specs=pl.BlockSpec((nt, SSM_GROUP, ncs, CHUNK), ugrp),
        out_shape=jax.ShapeDtypeStruct(ut4.shape, F32),
        scratch_shapes=[
            pltpu.VMEM((SSM_GROUP * SSM_GROUP, CHUNK), F32),
            pltpu.VMEM((SSM_GROUP * CHUNK, 2 * SSM_STATE), BF16),
            pltpu.VMEM((SSM_GROUP * CHUNK, 2 * SSM_STATE), BF16),
            pltpu.VMEM((SSM_GROUP * CHUNK // MXU_DIM, SSM_GROUP * CHUNK, MXU_DIM), BF16),
        ],
        compiler_params=pltpu.CompilerParams(
            dimension_semantics=("parallel",), vmem_limit_bytes=VMEM_LIMIT_BYTES),
        name="s5",
    )(ut4, pwk, pst, sm)


def _mlp_kernel(x_ref, yret_ref, yst_ref, wglu_ref, wout_ref, gpost_ref, gpre2_ref, gpost2_ref,
                w1_ref, w2_ref, o_ref, *, ff_chunk):
    ncs = x_ref.shape[0] // CHUNK
    yst = jnp.concatenate([yst_ref[0, pl.ds(c, SSM_WIDTH, stride=ncs), :].astype(BF16) for c in range(ncs)],
                          axis=1)
    glu = _dot_tn(yst, wglu_ref[...])
    yssm = (glu[:, :SSM_WIDTH] * jax.nn.sigmoid(glu[:, SSM_WIDTH:])).astype(BF16)
    mix = _dot(yret_ref[...], wout_ref[:RET_WIDTH, :]) + _dot(yssm, wout_ref[RET_WIDTH:, :])
    x1 = x_ref[...] + _rms(mix, gpost_ref[...])
    h = _rms(x1, gpre2_ref[...]).astype(BF16)
    mm = None
    for c in range(D_FF // ff_chunk):
        f = jnp.maximum(_dot(h, w1_ref[:, c * ff_chunk:(c + 1) * ff_chunk]), 0.0)
        part = _dot((f * f).astype(BF16), w2_ref[c * ff_chunk:(c + 1) * ff_chunk, :])
        mm = part if mm is None else mm + part
    o_ref[...] = x1 + _rms(mm, gpost2_ref[...])


def _mlp(x2, yret, yst, wglu, wout, gpost, gpre2, gpost2, w1, w2, tm, ff_chunk):
    n = x2.shape[0]
    const = lambda i: (0, 0)
    once = pl.Buffered(1)
    return pl.pallas_call(
        functools.partial(_mlp_kernel, ff_chunk=ff_chunk),
        grid=(n // tm,),
        in_specs=[
            pl.BlockSpec((tm, D_MODEL), lambda i: (i, 0)),
            pl.BlockSpec((tm, RET_WIDTH), lambda i: (i, 0)),
            pl.BlockSpec((1, SSM_WIDTH * (tm // CHUNK), CHUNK), lambda i: (i, 0, 0)),
            pl.BlockSpec((SSM_WIDTH, 2 * SSM_WIDTH), const, pipeline_mode=once),
            pl.BlockSpec((D_MODEL, D_MODEL), const, pipeline_mode=once),
            pl.BlockSpec((1, D_MODEL), const),
            pl.BlockSpec((1, D_MODEL), const),
            pl.BlockSpec((1, D_MODEL), const),
            pl.BlockSpec((D_MODEL, D_FF), const, pipeline_mode=once),
            pl.BlockSpec((D_FF, D_MODEL), const, pipeline_mode=once),
        ],
        out_specs=pl.BlockSpec((tm, D_MODEL), lambda i: (i, 0)),
        out_shape=jax.ShapeDtypeStruct((n, D_MODEL), F32),
        compiler_params=pltpu.CompilerParams(
            dimension_semantics=("parallel",), vmem_limit_bytes=VMEM_LIMIT_BYTES),
        name="mlp",
    )(x2, yret, yst, wglu, wout, gpost, gpre2, gpost2, w1, w2)


def _rope_tables(seq_len):
    half = HEAD_DIM // 2
    inv_freq = ROPE_BASE ** (-jnp.arange(half, dtype=F32) / half)
    ang = jnp.arange(seq_len, dtype=F32)[:, None] * inv_freq[None, :]
    cos, sin = jnp.cos(ang), jnp.sin(ang)
    cosf = jnp.concatenate([cos, cos], axis=1)
    sinf = jnp.concatenate([-sin, sin], axis=1)
    kscale = HEAD_DIM ** -0.5
    return cosf, sinf, cosf.T * kscale, sinf.T * kscale


def _retention_tables(tm):
    log_gamma = jnp.log(1.0 - jnp.exp(jnp.linspace(math.log(1.0 / 32), math.log(1.0 / 512), RET_HEADS))).astype(F32)
    idx = jnp.arange(CHUNK, dtype=F32)
    xiq = jnp.exp((idx + 1.0 - CHUNK)[None, :] * log_gamma[:, None])
    zeta = jnp.exp((CHUNK - 1 - idx)[None, :] * log_gamma[:, None])
    gch = jnp.exp(CHUNK * log_gamma)
    xiq_b = jnp.broadcast_to(xiq[:, :, None], (RET_HEADS, CHUNK, HEAD_DIM))
    ztt = jnp.tile(zeta, (1, tm // CHUNK))[:, None, :]
    gch_b = jnp.broadcast_to(gch[:, None, None], (RET_HEADS, 1, HEAD_DIM))
    return xiq_b, ztt, gch_b


def _s5_tables(lam_re, lam_im, log_dt, b_re, b_im, c_re, c_im, d_skip, nlev):
    assert nlev <= MAX_SCAN_LEVELS
    a = jnp.minimum(lam_re.astype(F32), -1e-4)
    b = lam_im.astype(F32)
    dt = jnp.exp(log_dt.astype(F32))[:, None]
    tau = jnp.arange(CHUNK + 1, dtype=F32)
    mag = jnp.exp((a * dt)[:, :, None] * tau)
    ph = (b * dt)[:, :, None] * tau
    pr, pi = mag * jnp.cos(ph), mag * jnp.sin(ph)
    lr1, li = pr[:, :, 1] - 1.0, pi[:, :, 1]
    den = a * a + b * b
    c0r, c0i = (lr1 * a + li * b) / den, (li * a - lr1 * b) / den
    bre, bim = b_re.astype(F32), b_im.astype(F32)
    br = jnp.swapaxes(c0r[:, :, None] * bre - c0i[:, :, None] * bim, 1, 2)
    bi = jnp.swapaxes(c0r[:, :, None] * bim + c0i[:, :, None] * bre, 1, 2)
    cr, ci = c_re.astype(F32), c_im.astype(F32)

    pwk = jnp.concatenate([pr[:, :, :CHUNK], pi[:, :, :CHUNK]], axis=1)
    prr = jnp.swapaxes(pr[:, :, CHUNK - 1::-1], 1, 2)
    pir = jnp.swapaxes(pi[:, :, CHUNK - 1::-1], 1, 2)
    pr1 = jnp.swapaxes(pr[:, :, 1:], 1, 2)
    pi1 = jnp.swapaxes(pi[:, :, 1:], 1, 2)
    cat = lambda x, y: jnp.concatenate([x, y], axis=-1)
    pst = jnp.stack([cat(prr, pir), cat(-pir, prr), cat(pr1, -pi1), cat(-pi1, -pr1)], axis=1)

    lam_rows = []
    for lev in range(MAX_SCAN_LEVELS):
        step = float(CHUNK * (1 << lev))
        mg = jnp.exp(a * dt * step)
        re, im = mg * jnp.cos(b * dt * step), mg * jnp.sin(b * dt * step)
        lam_rows += [cat(re, re), cat(-im, im)]
    lamtab = jnp.stack(lam_rows, axis=1)
    dsk = jnp.broadcast_to(d_skip.astype(F32).reshape(SSM_GROUPS, SSM_GROUP, 1), (SSM_GROUPS, SSM_GROUP, LANES))
    sm = jnp.concatenate([cat(br, br), cat(bi, bi), cat(br, -bi), cat(-bi, -br), cat(cr, cr), cat(ci, ci),
                          lamtab, dsk], axis=1)
    return pwk, pst, sm


def _tiles(seq_len):
    tm_mix = min(1024, seq_len)
    tm_mlp = tm_mix
    ff_chunk = 1024
    assert seq_len % CHUNK == 0 and seq_len % tm_mix == 0
    return tm_mix, tm_mlp, ff_chunk


def kernel(x, norm_mix_pre, norm_mix_post, w_in, ret_gn_gain, ssm_lambda_re, ssm_lambda_im, ssm_log_dt,
           ssm_b_re, ssm_b_im, ssm_c_re, ssm_c_im, ssm_d, w_glu, w_out, norm_mlp_pre, norm_mlp_post,
           w_ff1, w_ff2):
    batch, seq_len, _ = x.shape
    depth = w_in.shape[0]
    n = batch * seq_len
    nc = seq_len // CHUNK
    nlev = max(1, (nc - 1).bit_length())
    tm_mix, tm_mlp, ff_chunk = _tiles(seq_len)

    cosf, sinf, cost, sint = _rope_tables(seq_len)
    xiq, ztt, gch = _retention_tables(tm_mix)
    x2 = x.reshape(n, D_MODEL)
    for i in range(depth):
        wi = w_in[i]
        wqvg = jnp.concatenate([wi[:, :RET_WIDTH], wi[:, 2 * RET_WIDTH:4 * RET_WIDTH]], axis=1).astype(BF16)
        wkut = jnp.concatenate([wi[:, RET_WIDTH:2 * RET_WIDTH], wi[:, 4 * RET_WIDTH:]], axis=1).T.astype(BF16)
        gn = ret_gn_gain[i].astype(F32).reshape(RET_HEADS, 1, HEAD_DIM)
        yret, ut = _mix_in(x2, norm_mix_pre[i][None, :], wqvg, wkut, cosf, sinf, cost, sint,
                           xiq, ztt, gch, gn, batch, seq_len, tm_mix)
        pwk, pst, sm = _s5_tables(
            ssm_lambda_re[i], ssm_lambda_im[i], ssm_log_dt[i], ssm_b_re[i], ssm_b_im[i],
            ssm_c_re[i], ssm_c_im[i], ssm_d[i], nlev)
        ncs = tm_mix // CHUNK
        yst = _s5(ut.reshape(n // tm_mix, SSM_WIDTH, ncs, CHUNK), pwk, pst, sm, nc, nlev)
        yst = yst.reshape(n // tm_mlp, SSM_WIDTH * ncs, CHUNK)
        x2 = _mlp(x2, yret, yst, w_glu[i].astype(BF16), w_out[i].astype(BF16),
                  norm_mix_post[i][None, :], norm_mlp_pre[i][None, :], norm_mlp_post[i][None, :],
                  w_ff1[i].astype(BF16), w_ff2[i].astype(BF16), tm_mlp, ff_chunk)
    return x2.reshape(batch, seq_len, D_MODEL)
```

```python
import functools
import math

import jax
import jax.numpy as jnp
import numpy as np
from jax import lax
from jax.experimental import pallas as pl
from jax.experimental.pallas import tpu as pltpu

D_MODEL = 1024
RET_WIDTH = 512
RET_HEADS = 4
HEAD_DIM = 128
CHUNK = 128
ROPE_BASE = 10000.0
SSM_WIDTH = 512
SSM_GROUP = 16
SSM_GROUPS = 32
SSM_STATE = 64
D_FF = 4096
NORM_EPS = 1e-6
LANES = 128
MXU_DIM = 256

BF16 = jnp.bfloat16
F32 = jnp.float32

VMEM_LIMIT_BYTES = 56 * 1024 * 1024

ROW_BRR, ROW_BII, ROW_BB1, ROW_BB2, ROW_RC1, ROW_RC2, ROW_LAM, ROW_DSK = 0, 16, 32, 48, 64, 80, 96, 112
MAX_SCAN_LEVELS = (ROW_DSK - ROW_LAM) // 2


def _dot(a, b):
    return jnp.dot(a, b, preferred_element_type=F32)


def _dot_nt(a, b):
    return lax.dot_general(a, b, (((1,), (1,)), ((), ())), preferred_element_type=F32)


def _dot_tn(a, b):
    return lax.dot_general(a, b, (((0,), (0,)), ((), ())), preferred_element_type=F32)


def _rms(x, g):
    ms = jnp.mean(x * x, axis=-1, keepdims=True)
    return x * lax.rsqrt(ms + NORM_EPS) * g


def _mix_in_kernel(x_ref, g_ref, wqvg_ref, wkut_ref, cosf_ref, sinf_ref, cost_ref, sint_ref,
                   xiq_ref, ztt_ref, gch_ref, gn_ref, yret_ref, ut_ref, r_ref):
    @pl.when(pl.program_id(1) == 0)
    def _():
        r_ref[...] = jnp.zeros_like(r_ref)

    tm = x_ref.shape[0]
    half = HEAD_DIM // 2
    h = _rms(x_ref[...], g_ref[...]).astype(BF16)
    p = _dot(h, wqvg_ref[...])
    pt = _dot_nt(wkut_ref[...], h)
    ncs = tm // CHUNK
    for c in range(ncs):
        ut_ref[0, pl.ds(c, SSM_WIDTH, stride=ncs), :] = pt[RET_WIDTH:, c * CHUNK:(c + 1) * CHUNK]

    row = lax.broadcasted_iota(jnp.int32, (CHUNK, CHUNK), 0)
    col = lax.broadcasted_iota(jnp.int32, (CHUNK, CHUNK), 1)
    causal = row >= col
    cosf, sinf = cosf_ref[...], sinf_ref[...]
    cost, sint = cost_ref[...], sint_ref[...]
    for hd in range(RET_HEADS):
        sl = slice(hd * HEAD_DIM, (hd + 1) * HEAD_DIM)
        qh = p[:, sl]
        qh = qh * cosf + pltpu.roll(qh, half, 1) * sinf
        kh = pt[sl, :]
        sw = jnp.concatenate([kh[half:, :], kh[:half, :]], axis=0)
        kh = ((kh * cost + sw * sint) * ztt_ref[hd]).astype(BF16)
        g_state = r_ref[hd]
        for c in range(tm // CHUNK):
            rows = slice(c * CHUNK, (c + 1) * CHUNK)
            qc = (qh[rows, :] * xiq_ref[hd]).astype(BF16)
            kc = kh[:, rows]
            vc = p[rows, RET_WIDTH + hd * HEAD_DIM:RET_WIDTH + (hd + 1) * HEAD_DIM].astype(BF16)
            s = jnp.where(causal, _dot(qc, kc), 0.0).astype(BF16)
            y = _dot(s, vc) + _dot(qc, g_state.astype(BF16))
            g_state = gch_ref[hd] * (g_state + _dot(kc, vc))
            mu = jnp.mean(y, axis=-1, keepdims=True)
            yc = y - mu
            var = jnp.mean(yc * yc, axis=-1, keepdims=True)
            yn = yc * lax.rsqrt(var + NORM_EPS) * gn_ref[hd]
            gate = p[rows, 2 * RET_WIDTH + hd * HEAD_DIM:2 * RET_WIDTH + (hd + 1) * HEAD_DIM]
            yret_ref[rows, sl] = (jax.nn.silu(gate) * yn).astype(BF16)
        r_ref[hd] = g_state


def _mix_in(x2, g_pre, wqvg, wkut, cosf, sinf, cost, sint, xiq, ztt, gch, gn, batch, seq_len, tm):
    n = x2.shape[0]
    nseq = seq_len // tm
    tok = lambda b, j: (b * nseq + j, 0)
    tile3 = lambda b, j: (b * nseq + j, 0, 0)
    const2 = lambda b, j: (0, 0)
    const3 = lambda b, j: (0, 0, 0)
    return pl.pallas_call(
        _mix_in_kernel,
        grid=(batch, nseq),
        in_specs=[
            pl.BlockSpec((tm, D_MODEL), tok),
            pl.BlockSpec((1, D_MODEL), const2),
            pl.BlockSpec((D_MODEL, 3 * RET_WIDTH), const2),
            pl.BlockSpec((RET_WIDTH + SSM_WIDTH, D_MODEL), const2),
            pl.BlockSpec((tm, HEAD_DIM), lambda b, j: (j, 0)),
            pl.BlockSpec((tm, HEAD_DIM), lambda b, j: (j, 0)),
            pl.BlockSpec((HEAD_DIM, tm), lambda b, j: (0, j)),
            pl.BlockSpec((HEAD_DIM, tm), lambda b, j: (0, j)),
            pl.BlockSpec((RET_HEADS, CHUNK, HEAD_DIM), const3),
            pl.BlockSpec((RET_HEADS, 1, tm), const3),
            pl.BlockSpec((RET_HEADS, 1, HEAD_DIM), const3),
            pl.BlockSpec((RET_HEADS, 1, HEAD_DIM), const3),
        ],
        out_specs=[
            pl.BlockSpec((tm, RET_WIDTH), tok),
            pl.BlockSpec((1, SSM_WIDTH * (tm // CHUNK), CHUNK), tile3),
        ],
        out_shape=[
            jax.ShapeDtypeStruct((n, RET_WIDTH), BF16),
            jax.ShapeDtypeStruct((n // tm, SSM_WIDTH * (tm // CHUNK), CHUNK), F32),
        ],
        scratch_shapes=[pltpu.VMEM((RET_HEADS, HEAD_DIM, HEAD_DIM), F32)],
        compiler_params=pltpu.CompilerParams(
            dimension_semantics=("parallel", "arbitrary"), vmem_limit_bytes=VMEM_LIMIT_BYTES),
        name="mix_in",
    )(x2, g_pre, wqvg, wkut, cosf, sinf, cost, sint, xiq, ztt, gch, gn)


def _s5_kernel(u_ref, pw3_ref, sm_ref, y_ref, wtap_ref, wst_ref, wct_ref, ktoe_ref, *, nc, nlev):
    m = u_ref.shape[0] * u_ref.shape[2]
    nco = MXU_DIM // CHUNK

    def smrow(base, i):
        return sm_ref[0, base + i:base + i + 1, :]

    bb1 = sm_ref[0, ROW_BB1:ROW_BB1 + SSM_GROUP, :]
    bb2 = sm_ref[0, ROW_BB2:ROW_BB2 + SSM_GROUP, :]
    mcat = jnp.concatenate(
        [bb1 * smrow(ROW_RC1, co) + bb2 * smrow(ROW_RC2, co) for co in range(SSM_GROUP)], axis=0)
    taps = jnp.dot(mcat, pw3_ref[0, 0], precision=lax.Precision.HIGHEST, preferred_element_type=F32)
    wtap_ref[...] = taps

    lane = lax.broadcasted_iota(jnp.int32, (CHUNK, LANES), 1)
    sgn = jnp.where(lane < SSM_STATE, -1.0, 1.0)
    a1 = pw3_ref[0, 2].T
    a2 = pltpu.roll(a1, SSM_STATE, 1) * sgn
    for ci in range(SSM_GROUP):
        wst_ref[ci * CHUNK:(ci + 1) * CHUNK, :] = (
            a1 * smrow(ROW_BRR, ci) + a2 * smrow(ROW_BII, ci)).astype(BF16)
    p1t = pw3_ref[0, 1].T
    b1 = p1t * (-sgn)
    b2 = -pltpu.roll(p1t, SSM_STATE, 1)
    for co in range(SSM_GROUP):
        wct_ref[co * CHUNK:(co + 1) * CHUNK, :] = (
            b1 * smrow(ROW_RC1, co) + b2 * smrow(ROW_RC2, co)).astype(BF16)

    u = [u_ref[:, ci].reshape(m, CHUNK) for ci in range(SSM_GROUP)]
    a = jnp.concatenate([uc.astype(BF16) for uc in u], axis=1)
    st = _dot(a, wst_ref[...])

    chunk_id = lax.broadcasted_iota(jnp.int32, (m, LANES), 0) % nc
    for lev in range(nlev):
        k = 1 << lev
        sh = jnp.where(chunk_id >= k, pltpu.roll(st, k, 0), 0.0)
        st = st + sh * smrow(ROW_LAM, 2 * lev) + pltpu.roll(sh, SSM_STATE, 1) * smrow(ROW_LAM, 2 * lev + 1)
    xprev = jnp.where(chunk_id >= 1, pltpu.roll(st, 1, 0), 0.0).astype(BF16)

    trow = lax.broadcasted_iota(jnp.int32, (CHUNK, CHUNK), 0)
    tcol = lax.broadcasted_iota(jnp.int32, (CHUNK, CHUNK), 1)
    valid = tcol >= trow
    for j in range(SSM_GROUP // nco):
        for cc in range(nco):
            co = j * nco + cc
            for ci in range(SSM_GROUP):
                tile = jnp.broadcast_to(wtap_ref[co * SSM_GROUP + ci:co * SSM_GROUP + ci + 1, :],
                                        (CHUNK, CHUNK))
                tile = pltpu.roll(tile, 0, 1, stride=1, stride_axis=0)
                ktoe_ref[j, ci * CHUNK:(ci + 1) * CHUNK, cc * CHUNK:(cc + 1) * CHUNK] = (
                    jnp.where(valid, tile, 0.0).astype(BF16))
        yj = _dot(a, ktoe_ref[j]) + _dot_nt(xprev, wct_ref[j * MXU_DIM:(j + 1) * MXU_DIM, :])
        for cc in range(nco):
            co = j * nco + cc
            yc = yj[:, cc * CHUNK:(cc + 1) * CHUNK] + smrow(ROW_DSK, co) * u[co]
            y_ref[:, co] = jax.nn.gelu(yc).reshape(y_ref.shape[0], y_ref.shape[2], CHUNK)


def _s5(ut4, pw3, sm, nc, nlev):
    nt, _, ncs, _ = ut4.shape
    grp = lambda g: (g, 0, 0)
    ugrp = lambda g: (0, g, 0, 0)
    return pl.pallas_call(
        functools.partial(_s5_kernel, nc=nc, nlev=nlev),
        grid=(SSM_GROUPS,),
        in_specs=[
            pl.BlockSpec((nt, SSM_GROUP, ncs, CHUNK), ugrp),
            pl.BlockSpec((1, 3, 2 * SSM_STATE, CHUNK), lambda g: (g, 0, 0, 0)),
            pl.BlockSpec((1, LANES, LANES), grp),
        ],
        out_specs=pl.BlockSpec((nt, SSM_GROUP, ncs, CHUNK), ugrp),
        out_shape=jax.ShapeDtypeStruct(ut4.shape, F32),
        scratch_shapes=[
            pltpu.VMEM((SSM_GROUP * SSM_GROUP, CHUNK), F32),
            pltpu.VMEM((SSM_GROUP * CHUNK, 2 * SSM_STATE), BF16),
            pltpu.VMEM((SSM_GROUP * CHUNK, 2 * SSM_STATE), BF16),
            pltpu.VMEM((SSM_GROUP * CHUNK // MXU_DIM, SSM_GROUP * CHUNK, MXU_DIM), BF16),
        ],
        compiler_params=pltpu.CompilerParams(
            dimension_semantics=("parallel",), vmem_limit_bytes=VMEM_LIMIT_BYTES),
        name="s5",
    )(ut4, pw3, sm)


def _mlp_kernel(x_ref, yret_ref, yst_ref, wglu_ref, wout_ref, gpost_ref, gpre2_ref, gpost2_ref,
                w1_ref, w2_ref, o_ref, *, ff_chunk):
    ncs = x_ref.shape[0] // CHUNK
    yst = jnp.concatenate([yst_ref[0, pl.ds(c, SSM_WIDTH, stride=ncs), :].astype(BF16) for c in range(ncs)],
                          axis=1)
    glu = _dot_tn(yst, wglu_ref[...])
    yssm = (glu[:, :SSM_WIDTH] * jax.nn.sigmoid(glu[:, SSM_WIDTH:])).astype(BF16)
    mix = _dot(yret_ref[...], wout_ref[:RET_WIDTH, :]) + _dot(yssm, wout_ref[RET_WIDTH:, :])
    x1 = x_ref[...] + _rms(mix, gpost_ref[...])
    h = _rms(x1, gpre2_ref[...]).astype(BF16)
    mm = None
    for c in range(D_FF // ff_chunk):
        f = jnp.maximum(_dot(h, w1_ref[:, c * ff_chunk:(c + 1) * ff_chunk]), 0.0)
        part = _dot((f * f).astype(BF16), w2_ref[c * ff_chunk:(c + 1) * ff_chunk, :])
        mm = part if mm is None else mm + part
    o_ref[...] = x1 + _rms(mm, gpost2_ref[...])


def _mlp(x2, yret, yst, wglu, wout, gpost, gpre2, gpost2, w1, w2, tm, ff_chunk):
    n = x2.shape[0]
    const = lambda i: (0, 0)
    once = pl.Buffered(1)
    return pl.pallas_call(
        functools.partial(_mlp_kernel, ff_chunk=ff_chunk),
        grid=(n // tm,),
        in_specs=[
            pl.BlockSpec((tm, D_MODEL), lambda i: (i, 0)),
            pl.BlockSpec((tm, RET_WIDTH), lambda i: (i, 0)),
            pl.BlockSpec((1, SSM_WIDTH * (tm // CHUNK), CHUNK), lambda i: (i, 0, 0)),
            pl.BlockSpec((SSM_WIDTH, 2 * SSM_WIDTH), const, pipeline_mode=once),
            pl.BlockSpec((D_MODEL, D_MODEL), const, pipeline_mode=once),
            pl.BlockSpec((1, D_MODEL), const),
            pl.BlockSpec((1, D_MODEL), const),
            pl.BlockSpec((1, D_MODEL), const),
            pl.BlockSpec((D_MODEL, D_FF), const, pipeline_mode=once),
            pl.BlockSpec((D_FF, D_MODEL), const, pipeline_mode=once),
        ],
        out_specs=pl.BlockSpec((tm, D_MODEL), lambda i: (i, 0)),
        out_shape=jax.ShapeDtypeStruct((n, D_MODEL), F32),
        compiler_params=pltpu.CompilerParams(
            dimension_semantics=("parallel",), vmem_limit_bytes=VMEM_LIMIT_BYTES),
        name="mlp",
    )(x2, yret, yst, wglu, wout, gpost, gpre2, gpost2, w1, w2)


def _rope_tables(seq_len):
    half = HEAD_DIM // 2
    inv_freq = ROPE_BASE ** (-np.arange(half, dtype=np.float64) / half)
    ang = np.arange(seq_len, dtype=np.float64)[:, None] * inv_freq[None, :]
    cos, sin = np.cos(ang), np.sin(ang)
    cosf = np.concatenate([cos, cos], axis=1)
    sinf = np.concatenate([-sin, sin], axis=1)
    kscale = HEAD_DIM ** -0.5
    f32 = lambda t: jnp.asarray(np.ascontiguousarray(t), dtype=F32)
    return f32(cosf), f32(sinf), f32(cosf.T * kscale), f32(sinf.T * kscale)


def _retention_tables(tm):
    log_gamma = np.log(1.0 - np.exp(np.linspace(math.log(1.0 / 32), math.log(1.0 / 512), RET_HEADS)))
    idx = np.arange(CHUNK, dtype=np.float64)
    xiq = np.exp((idx + 1.0 - CHUNK)[None, :] * log_gamma[:, None])
    zeta = np.exp((CHUNK - 1 - idx)[None, :] * log_gamma[:, None])
    gch = np.exp(CHUNK * log_gamma)
    xiq_b = np.broadcast_to(xiq[:, :, None], (RET_HEADS, CHUNK, HEAD_DIM))
    ztt = np.tile(zeta, (1, tm // CHUNK))[:, None, :]
    gch_b = np.broadcast_to(gch[:, None, None], (RET_HEADS, 1, HEAD_DIM))
    f32 = lambda t: jnp.asarray(np.ascontiguousarray(t), dtype=F32)
    return f32(xiq_b), f32(ztt), f32(gch_b)


def _s5_tables(lam_re, lam_im, log_dt, b_re, b_im, c_re, c_im, d_skip, nlev):
    assert nlev <= MAX_SCAN_LEVELS
    a = jnp.minimum(lam_re.astype(F32), -1e-4)
    b = lam_im.astype(F32)
    dt = jnp.exp(log_dt.astype(F32))[:, None]
    tau = jnp.arange(CHUNK + 1, dtype=F32)
    mag = jnp.exp((a * dt)[:, :, None] * tau)
    ph = (b * dt)[:, :, None] * tau
    pr, pi = mag * jnp.cos(ph), mag * jnp.sin(ph)
    lr1, li = pr[:, :, 1] - 1.0, pi[:, :, 1]
    den = a * a + b * b
    c0r, c0i = (lr1 * a + li * b) / den, (li * a - lr1 * b) / den
    bre, bim = b_re.astype(F32), b_im.astype(F32)
    br = jnp.swapaxes(c0r[:, :, None] * bre - c0i[:, :, None] * bim, 1, 2)
    bi = jnp.swapaxes(c0r[:, :, None] * bim + c0i[:, :, None] * bre, 1, 2)
    cr, ci = c_re.astype(F32), c_im.astype(F32)

    pall = jnp.concatenate([pr, pi], axis=1)
    pw3 = jnp.stack([pall[:, :, :CHUNK], pall[:, :, 1:], pall[:, :, CHUNK - 1::-1]], axis=1)
    cat = lambda x, y: jnp.concatenate([x, y], axis=-1)

    lam_rows = []
    for lev in range(MAX_SCAN_LEVELS):
        step = float(CHUNK * (1 << lev))
        mg = jnp.exp(a * dt * step)
        re, im = mg * jnp.cos(b * dt * step), mg * jnp.sin(b * dt * step)
        lam_rows += [cat(re, re), cat(-im, im)]
    lamtab = jnp.stack(lam_rows, axis=1)
    dsk = jnp.broadcast_to(d_skip.astype(F32).reshape(SSM_GROUPS, SSM_GROUP, 1), (SSM_GROUPS, SSM_GROUP, LANES))
    sm = jnp.concatenate([cat(br, br), cat(bi, bi), cat(br, -bi), cat(-bi, -br), cat(cr, cr), cat(ci, ci),
                          lamtab, dsk], axis=1)
    return pw3, sm


def _tiles(seq_len):
    tm_mix = min(1024, seq_len)
    tm_mlp = tm_mix
    ff_chunk = 1024
    assert seq_len % CHUNK == 0 and seq_len % tm_mix == 0
    return tm_mix, tm_mlp, ff_chunk


def kernel(x, norm_mix_pre, norm_mix_post, w_in, ret_gn_gain, ssm_lambda_re, ssm_lambda_im, ssm_log_dt,
           ssm_b_re, ssm_b_im, ssm_c_re, ssm_c_im, ssm_d, w_glu, w_out, norm_mlp_pre, norm_mlp_post,
           w_ff1, w_ff2):
    batch, seq_len, _ = x.shape
    depth = w_in.shape[0]
    n = batch * seq_len
    nc = seq_len // CHUNK
    nlev = max(1, (nc - 1).bit_length())
    tm_mix, tm_mlp, ff_chunk = _tiles(seq_len)

    cosf, sinf, cost, sint = _rope_tables(seq_len)
    xiq, ztt, gch = _retention_tables(tm_mix)
    x2 = x.reshape(n, D_MODEL)
    for i in range(depth):
        wi = w_in[i]
        wqvg = jnp.concatenate([wi[:, :RET_WIDTH], wi[:, 2 * RET_WIDTH:4 * RET_WIDTH]], axis=1).astype(BF16)
        wkut = jnp.concatenate([wi[:, RET_WIDTH:2 * RET_WIDTH], wi[:, 4 * RET_WIDTH:]], axis=1).T.astype(BF16)
        gn = ret_gn_gain[i].astype(F32).reshape(RET_HEADS, 1, HEAD_DIM)
        yret, ut = _mix_in(x2, norm_mix_pre[i][None, :], wqvg, wkut, cosf, sinf, cost, sint,
                           xiq, ztt, gch, gn, batch, seq_len, tm_mix)
        pw3, sm = _s5_tables(
            ssm_lambda_re[i], ssm_lambda_im[i], ssm_log_dt[i], ssm_b_re[i], ssm_b_im[i],
            ssm_c_re[i], ssm_c_im[i], ssm_d[i], nlev)
        ncs = tm_mix // CHUNK
        yst = _s5(ut.reshape(n // tm_mix, SSM_WIDTH, ncs, CHUNK), pw3, sm, nc, nlev)
        yst = yst.reshape(n // tm_mlp, SSM_WIDTH * ncs, CHUNK)
        x2 = _mlp(x2, yret, yst, w_glu[i].astype(BF16), w_out[i].astype(BF16),
                  norm_mix_post[i][None, :], norm_mlp_pre[i][None, :], norm_mlp_post[i][None, :],
                  w_ff1[i].astype(BF16), w_ff2[i].astype(BF16), tm_mlp, ff_chunk)
    return x2.reshape(batch, seq_len, D_MODEL)
```

```python
import functools
import math

import jax
import jax.numpy as jnp
import numpy as np
from jax import lax
from jax.experimental import pallas as pl
from jax.experimental.pallas import tpu as pltpu

D_MODEL = 1024
RET_WIDTH = 512
RET_HEADS = 4
HEAD_DIM = 128
CHUNK = 128
ROPE_BASE = 10000.0
SSM_WIDTH = 512
SSM_GROUP = 16
SSM_GROUPS = 32
SSM_STATE = 64
D_FF = 4096
NORM_EPS = 1e-6
LANES = 128
MXU_DIM = 256
MLP_ROW_SPLITS = 2

BF16 = jnp.bfloat16
F32 = jnp.float32

VMEM_LIMIT_BYTES = 56 * 1024 * 1024

ROW_BRR, ROW_BII, ROW_BB1, ROW_BB2, ROW_RC1, ROW_RC2, ROW_LAM, ROW_DSK = 0, 16, 32, 48, 64, 80, 96, 112
MAX_SCAN_LEVELS = (ROW_DSK - ROW_LAM) // 2


def _dot(a, b):
    return jnp.dot(a, b, preferred_element_type=F32)


def _dot_nt(a, b):
    return lax.dot_general(a, b, (((1,), (1,)), ((), ())), preferred_element_type=F32)


def _dot_tn(a, b):
    return lax.dot_general(a, b, (((0,), (0,)), ((), ())), preferred_element_type=F32)


def _rms(x, g):
    ms = jnp.mean(x * x, axis=-1, keepdims=True)
    return x * lax.rsqrt(ms + NORM_EPS) * g


def _mix_in_kernel(x_ref, g_ref, wqvg_ref, wkut_ref, cosf_ref, sinf_ref, cost_ref, sint_ref,
                   xiq_ref, ztt_ref, gch_ref, gn_ref, yret_ref, ut_ref, r_ref):
    @pl.when(pl.program_id(1) == 0)
    def _():
        r_ref[...] = jnp.zeros_like(r_ref)

    tm = x_ref.shape[0]
    half = HEAD_DIM // 2
    h = _rms(x_ref[...], g_ref[...]).astype(BF16)
    p = _dot(h, wqvg_ref[...])
    pt = _dot_nt(wkut_ref[...], h)
    ncs = tm // CHUNK
    for c in range(ncs):
        ut_ref[0, pl.ds(c, SSM_WIDTH, stride=ncs), :] = pt[RET_WIDTH:, c * CHUNK:(c + 1) * CHUNK]

    row = lax.broadcasted_iota(jnp.int32, (CHUNK, CHUNK), 0)
    col = lax.broadcasted_iota(jnp.int32, (CHUNK, CHUNK), 1)
    causal = row >= col
    cosf, sinf = cosf_ref[...], sinf_ref[...]
    cost, sint = cost_ref[...], sint_ref[...]
    for hd in range(RET_HEADS):
        sl = slice(hd * HEAD_DIM, (hd + 1) * HEAD_DIM)
        qh = p[:, sl]
        qh = qh * cosf + pltpu.roll(qh, half, 1) * sinf
        kh = pt[sl, :]
        sw = jnp.concatenate([kh[half:, :], kh[:half, :]], axis=0)
        kh = ((kh * cost + sw * sint) * ztt_ref[hd]).astype(BF16)
        g_state = r_ref[hd]
        for c in range(tm // CHUNK):
            rows = slice(c * CHUNK, (c + 1) * CHUNK)
            qc = (qh[rows, :] * xiq_ref[hd]).astype(BF16)
            kc = kh[:, rows]
            vc = p[rows, RET_WIDTH + hd * HEAD_DIM:RET_WIDTH + (hd + 1) * HEAD_DIM].astype(BF16)
            s = jnp.where(causal, _dot(qc, kc), 0.0).astype(BF16)
            y = _dot(jnp.concatenate([s, qc], axis=1), jnp.concatenate([vc, g_state.astype(BF16)], axis=0))
            g_state = gch_ref[hd] * (g_state + _dot(kc, vc))
            mu = jnp.mean(y, axis=-1, keepdims=True)
            yc = y - mu
            var = jnp.mean(yc * yc, axis=-1, keepdims=True)
            yn = yc * lax.rsqrt(var + NORM_EPS) * gn_ref[hd]
            gate = p[rows, 2 * RET_WIDTH + hd * HEAD_DIM:2 * RET_WIDTH + (hd + 1) * HEAD_DIM]
            yret_ref[rows, sl] = (jax.nn.silu(gate) * yn).astype(BF16)
        r_ref[hd] = g_state


def _mix_in(x2, g_pre, wqvg, wkut, cosf, sinf, cost, sint, xiq, ztt, gch, gn, batch, seq_len, tm):
    n = x2.shape[0]
    nseq = seq_len // tm
    tok = lambda b, j: (b * nseq + j, 0)
    tile3 = lambda b, j: (b * nseq + j, 0, 0)
    const2 = lambda b, j: (0, 0)
    const3 = lambda b, j: (0, 0, 0)
    return pl.pallas_call(
        _mix_in_kernel,
        grid=(batch, nseq),
        in_specs=[
            pl.BlockSpec((tm, D_MODEL), tok),
            pl.BlockSpec((1, D_MODEL), const2),
            pl.BlockSpec((D_MODEL, 3 * RET_WIDTH), const2),
            pl.BlockSpec((RET_WIDTH + SSM_WIDTH, D_MODEL), const2),
            pl.BlockSpec((tm, HEAD_DIM), lambda b, j: (j, 0)),
            pl.BlockSpec((tm, HEAD_DIM), lambda b, j: (j, 0)),
            pl.BlockSpec((HEAD_DIM, tm), lambda b, j: (0, j)),
            pl.BlockSpec((HEAD_DIM, tm), lambda b, j: (0, j)),
            pl.BlockSpec((RET_HEADS, CHUNK, HEAD_DIM), const3),
            pl.BlockSpec((RET_HEADS, 1, tm), const3),
            pl.BlockSpec((RET_HEADS, 1, HEAD_DIM), const3),
            pl.BlockSpec((RET_HEADS, 1, HEAD_DIM), const3),
        ],
        out_specs=[
            pl.BlockSpec((tm, RET_WIDTH), tok),
            pl.BlockSpec((1, SSM_WIDTH * (tm // CHUNK), CHUNK), tile3),
        ],
        out_shape=[
            jax.ShapeDtypeStruct((n, RET_WIDTH), BF16),
            jax.ShapeDtypeStruct((n // tm, SSM_WIDTH * (tm // CHUNK), CHUNK), F32),
        ],
        scratch_shapes=[pltpu.VMEM((RET_HEADS, HEAD_DIM, HEAD_DIM), F32)],
        compiler_params=pltpu.CompilerParams(
            dimension_semantics=("parallel", "arbitrary"), vmem_limit_bytes=VMEM_LIMIT_BYTES),
        name="mix_in",
    )(x2, g_pre, wqvg, wkut, cosf, sinf, cost, sint, xiq, ztt, gch, gn)


def _s5_kernel(u_ref, pw3_ref, sm_ref, y_ref, wtap_ref, wst_ref, wct_ref, ktoe_ref, *, nc, nlev):
    m = u_ref.shape[0] * u_ref.shape[2]
    nco = MXU_DIM // CHUNK

    def smrow(base, i):
        return sm_ref[0, base + i:base + i + 1, :]

    bb1 = sm_ref[0, ROW_BB1:ROW_BB1 + SSM_GROUP, :]
    bb2 = sm_ref[0, ROW_BB2:ROW_BB2 + SSM_GROUP, :]
    mcat = jnp.concatenate(
        [bb1 * smrow(ROW_RC1, co) + bb2 * smrow(ROW_RC2, co) for co in range(SSM_GROUP)], axis=0)
    taps = jnp.dot(mcat, pw3_ref[0, 0], precision=lax.Precision.HIGHEST, preferred_element_type=F32)
    wtap_ref[...] = taps

    lane = lax.broadcasted_iota(jnp.int32, (CHUNK, LANES), 1)
    sgn = jnp.where(lane < SSM_STATE, -1.0, 1.0)
    a1 = pw3_ref[0, 2].T
    a2 = pltpu.roll(a1, SSM_STATE, 1) * sgn
    for ci in range(SSM_GROUP):
        wst_ref[ci * CHUNK:(ci + 1) * CHUNK, :] = (
            a1 * smrow(ROW_BRR, ci) + a2 * smrow(ROW_BII, ci)).astype(BF16)
    p1t = pw3_ref[0, 1].T
    b1 = p1t * (-sgn)
    b2 = -pltpu.roll(p1t, SSM_STATE, 1)
    for co in range(SSM_GROUP):
        wct_ref[co * CHUNK:(co + 1) * CHUNK, :] = (
            b1 * smrow(ROW_RC1, co) + b2 * smrow(ROW_RC2, co)).astype(BF16)

    u = [u_ref[:, ci].reshape(m, CHUNK) for ci in range(SSM_GROUP)]
    a = jnp.concatenate([uc.astype(BF16) for uc in u], axis=1)
    st = _dot(a, wst_ref[...])

    chunk_id = lax.broadcasted_iota(jnp.int32, (m, LANES), 0) % nc
    for lev in range(nlev):
        k = 1 << lev
        sh = jnp.where(chunk_id >= k, pltpu.roll(st, k, 0), 0.0)
        st = st + sh * smrow(ROW_LAM, 2 * lev) + pltpu.roll(sh, SSM_STATE, 1) * smrow(ROW_LAM, 2 * lev + 1)
    xprev = jnp.where(chunk_id >= 1, pltpu.roll(st, 1, 0), 0.0).astype(BF16)

    trow = lax.broadcasted_iota(jnp.int32, (CHUNK, CHUNK), 0)
    tcol = lax.broadcasted_iota(jnp.int32, (CHUNK, CHUNK), 1)
    valid = tcol >= trow
    for j in range(SSM_GROUP // nco):
        for cc in range(nco):
            co = j * nco + cc
            for ci in range(SSM_GROUP):
                tile = jnp.broadcast_to(wtap_ref[co * SSM_GROUP + ci:co * SSM_GROUP + ci + 1, :],
                                        (CHUNK, CHUNK))
                tile = pltpu.roll(tile, 0, 1, stride=1, stride_axis=0)
                ktoe_ref[j, ci * CHUNK:(ci + 1) * CHUNK, cc * CHUNK:(cc + 1) * CHUNK] = (
                    jnp.where(valid, tile, 0.0).astype(BF16))
        yj = _dot(a, ktoe_ref[j]) + _dot_nt(xprev, wct_ref[j * MXU_DIM:(j + 1) * MXU_DIM, :])
        for cc in range(nco):
            co = j * nco + cc
            yc = yj[:, cc * CHUNK:(cc + 1) * CHUNK] + smrow(ROW_DSK, co) * u[co]
            y_ref[:, co] = jax.nn.gelu(yc).reshape(y_ref.shape[0], y_ref.shape[2], CHUNK)


def _s5(ut4, pw3, sm, nc, nlev):
    nt, _, ncs, _ = ut4.shape
    grp = lambda g: (g, 0, 0)
    ugrp = lambda g: (0, g, 0, 0)
    return pl.pallas_call(
        functools.partial(_s5_kernel, nc=nc, nlev=nlev),
        grid=(SSM_GROUPS,),
        in_specs=[
            pl.BlockSpec((nt, SSM_GROUP, ncs, CHUNK), ugrp),
            pl.BlockSpec((1, 3, 2 * SSM_STATE, CHUNK), lambda g: (g, 0, 0, 0)),
            pl.BlockSpec((1, LANES, LANES), grp),
        ],
        out_specs=pl.BlockSpec((nt, SSM_GROUP, ncs, CHUNK), ugrp),
        out_shape=jax.ShapeDtypeStruct(ut4.shape, F32),
        scratch_shapes=[
            pltpu.VMEM((SSM_GROUP * SSM_GROUP, CHUNK), F32),
            pltpu.VMEM((SSM_GROUP * CHUNK, 2 * SSM_STATE), BF16),
            pltpu.VMEM((SSM_GROUP * CHUNK, 2 * SSM_STATE), BF16),
            pltpu.VMEM((SSM_GROUP * CHUNK // MXU_DIM, SSM_GROUP * CHUNK, MXU_DIM), BF16),
        ],
        compiler_params=pltpu.CompilerParams(
            dimension_semantics=("parallel",), vmem_limit_bytes=VMEM_LIMIT_BYTES),
        name="s5",
    )(ut4, pw3, sm)


def _mlp_kernel(x_ref, yret_ref, yst_ref, wglu_ref, wout_ref, gpost_ref, gpre2_ref, gpost2_ref,
                w1_ref, w2_ref, o_ref, *, ff_chunk, row_splits):
    tm = x_ref.shape[0]
    ncs = tm // CHUNK
    rs, cps = tm // row_splits, ncs // row_splits
    x1s, hs = [], []
    for sp in range(row_splits):
        rows = slice(sp * rs, (sp + 1) * rs)
        yst = jnp.concatenate([yst_ref[0, pl.ds(c, SSM_WIDTH, stride=ncs), :].astype(BF16)
                               for c in range(sp * cps, (sp + 1) * cps)], axis=1)
        glu = _dot_tn(yst, wglu_ref[...])
        yssm = (glu[:, :SSM_WIDTH] * jax.nn.sigmoid(glu[:, SSM_WIDTH:])).astype(BF16)
        mix = _dot(yret_ref[rows, :], wout_ref[:RET_WIDTH, :]) + _dot(yssm, wout_ref[RET_WIDTH:, :])
        x1 = x_ref[rows, :] + _rms(mix, gpost_ref[...])
        x1s.append(x1)
        hs.append(_rms(x1, gpre2_ref[...]).astype(BF16))
    mms = [None] * row_splits
    for c in range(D_FF // ff_chunk):
        for sp in range(row_splits):
            f = jnp.maximum(_dot(hs[sp], w1_ref[:, c * ff_chunk:(c + 1) * ff_chunk]), 0.0)
            part = _dot((f * f).astype(BF16), w2_ref[c * ff_chunk:(c + 1) * ff_chunk, :])
            mms[sp] = part if mms[sp] is None else mms[sp] + part
    for sp in range(row_splits):
        o_ref[sp * rs:(sp + 1) * rs, :] = x1s[sp] + _rms(mms[sp], gpost2_ref[...])


def _mlp(x2, yret, yst, wglu, wout, gpost, gpre2, gpost2, w1, w2, tm, ff_chunk):
    n = x2.shape[0]
    const = lambda i: (0, 0)
    once = pl.Buffered(1)
    return pl.pallas_call(
        functools.partial(_mlp_kernel, ff_chunk=ff_chunk, row_splits=MLP_ROW_SPLITS),
        grid=(n // tm,),
        in_specs=[
            pl.BlockSpec((tm, D_MODEL), lambda i: (i, 0)),
            pl.BlockSpec((tm, RET_WIDTH), lambda i: (i, 0)),
            pl.BlockSpec((1, SSM_WIDTH * (tm // CHUNK), CHUNK), lambda i: (i, 0, 0)),
            pl.BlockSpec((SSM_WIDTH, 2 * SSM_WIDTH), const, pipeline_mode=once),
            pl.BlockSpec((D_MODEL, D_MODEL), const, pipeline_mode=once),
            pl.BlockSpec((1, D_MODEL), const),
            pl.BlockSpec((1, D_MODEL), const),
            pl.BlockSpec((1, D_MODEL), const),
            pl.BlockSpec((D_MODEL, D_FF), const, pipeline_mode=once),
            pl.BlockSpec((D_FF, D_MODEL), const, pipeline_mode=once),
        ],
        out_specs=pl.BlockSpec((tm, D_MODEL), lambda i: (i, 0)),
        out_shape=jax.ShapeDtypeStruct((n, D_MODEL), F32),
        compiler_params=pltpu.CompilerParams(
            dimension_semantics=("parallel",), vmem_limit_bytes=VMEM_LIMIT_BYTES),
        name="mlp",
    )(x2, yret, yst, wglu, wout, gpost, gpre2, gpost2, w1, w2)


def _rope_tables(seq_len):
    half = HEAD_DIM // 2
    inv_freq = ROPE_BASE ** (-np.arange(half, dtype=np.float64) / half)
    ang = np.arange(seq_len, dtype=np.float64)[:, None] * inv_freq[None, :]
    cos, sin = np.cos(ang), np.sin(ang)
    cosf = np.concatenate([cos, cos], axis=1)
    sinf = np.concatenate([-sin, sin], axis=1)
    kscale = HEAD_DIM ** -0.5
    f32 = lambda t: jnp.asarray(np.ascontiguousarray(t), dtype=F32)
    return f32(cosf), f32(sinf), f32(cosf.T * kscale), f32(sinf.T * kscale)


def _retention_tables(tm):
    log_gamma = np.log(1.0 - np.exp(np.linspace(math.log(1.0 / 32), math.log(1.0 / 512), RET_HEADS)))
    idx = np.arange(CHUNK, dtype=np.float64)
    xiq = np.exp((idx + 1.0 - CHUNK)[None, :] * log_gamma[:, None])
    zeta = np.exp((CHUNK - 1 - idx)[None, :] * log_gamma[:, None])
    gch = np.exp(CHUNK * log_gamma)
    xiq_b = np.broadcast_to(xiq[:, :, None], (RET_HEADS, CHUNK, HEAD_DIM))
    ztt = np.tile(zeta, (1, tm // CHUNK))[:, None, :]
    gch_b = np.broadcast_to(gch[:, None, None], (RET_HEADS, 1, HEAD_DIM))
    f32 = lambda t: jnp.asarray(np.ascontiguousarray(t), dtype=F32)
    return f32(xiq_b), f32(ztt), f32(gch_b)


def _s5_tables(lam_re, lam_im, log_dt, b_re, b_im, c_re, c_im, d_skip, nlev):
    assert nlev <= MAX_SCAN_LEVELS
    a = jnp.minimum(lam_re.astype(F32), -1e-4)
    b = lam_im.astype(F32)
    dt = jnp.exp(log_dt.astype(F32))[:, None]
    tau = jnp.arange(CHUNK + 1, dtype=F32)
    mag = jnp.exp((a * dt)[:, :, None] * tau)
    ph = (b * dt)[:, :, None] * tau
    pr, pi = mag * jnp.cos(ph), mag * jnp.sin(ph)
    lr1, li = pr[:, :, 1] - 1.0, pi[:, :, 1]
    den = a * a + b * b
    c0r, c0i = (lr1 * a + li * b) / den, (li * a - lr1 * b) / den
    bre, bim = b_re.astype(F32), b_im.astype(F32)
    br = jnp.swapaxes(c0r[:, :, None] * bre - c0i[:, :, None] * bim, 1, 2)
    bi = jnp.swapaxes(c0r[:, :, None] * bim + c0i[:, :, None] * bre, 1, 2)
    cr, ci = c_re.astype(F32), c_im.astype(F32)

    pall = jnp.concatenate([pr, pi], axis=1)
    pw3 = jnp.stack([pall[:, :, :CHUNK], pall[:, :, 1:], pall[:, :, CHUNK - 1::-1]], axis=1)
    cat = lambda x, y: jnp.concatenate([x, y], axis=-1)

    lam_rows = []
    for lev in range(MAX_SCAN_LEVELS):
        step = float(CHUNK * (1 << lev))
        mg = jnp.exp(a * dt * step)
        re, im = mg * jnp.cos(b * dt * step), mg * jnp.sin(b * dt * step)
        lam_rows += [cat(re, re), cat(-im, im)]
    lamtab = jnp.stack(lam_rows, axis=1)
    dsk = jnp.broadcast_to(d_skip.astype(F32).reshape(SSM_GROUPS, SSM_GROUP, 1), (SSM_GROUPS, SSM_GROUP, LANES))
    sm = jnp.concatenate([cat(br, br), cat(bi, bi), cat(br, -bi), cat(-bi, -br), cat(cr, cr), cat(ci, ci),
                          lamtab, dsk], axis=1)
    return pw3, sm


def _tiles(seq_len):
    tm_mix = min(1024, seq_len)
    tm_mlp = tm_mix
    ff_chunk = 1024
    assert seq_len % CHUNK == 0 and seq_len % tm_mix == 0
    return tm_mix, tm_mlp, ff_chunk


def kernel(x, norm_mix_pre, norm_mix_post, w_in, ret_gn_gain, ssm_lambda_re, ssm_lambda_im, ssm_log_dt,
           ssm_b_re, ssm_b_im, ssm_c_re, ssm_c_im, ssm_d, w_glu, w_out, norm_mlp_pre, norm_mlp_post,
           w_ff1, w_ff2):
    batch, seq_len, _ = x.shape
    depth = w_in.shape[0]
    n = batch * seq_len
    nc = seq_len // CHUNK
    nlev = max(1, (nc - 1).bit_length())
    tm_mix, tm_mlp, ff_chunk = _tiles(seq_len)

    cosf, sinf, cost, sint = _rope_tables(seq_len)
    xiq, ztt, gch = _retention_tables(tm_mix)
    x2 = x.reshape(n, D_MODEL)
    for i in range(depth):
        wi = w_in[i]
        wqvg = jnp.concatenate([wi[:, :RET_WIDTH], wi[:, 2 * RET_WIDTH:4 * RET_WIDTH]], axis=1).astype(BF16)
        wkut = jnp.concatenate([wi[:, RET_WIDTH:2 * RET_WIDTH], wi[:, 4 * RET_WIDTH:]], axis=1).T.astype(BF16)
        gn = ret_gn_gain[i].astype(F32).reshape(RET_HEADS, 1, HEAD_DIM)
        yret, ut = _mix_in(x2, norm_mix_pre[i][None, :], wqvg, wkut, cosf, sinf, cost, sint,
                           xiq, ztt, gch, gn, batch, seq_len, tm_mix)
        pw3, sm = _s5_tables(
            ssm_lambda_re[i], ssm_lambda_im[i], ssm_log_dt[i], ssm_b_re[i], ssm_b_im[i],
            ssm_c_re[i], ssm_c_im[i], ssm_d[i], nlev)
        ncs = tm_mix // CHUNK
        yst = _s5(ut.reshape(n // tm_mix, SSM_WIDTH, ncs, CHUNK), pw3, sm, nc, nlev)
        yst = yst.reshape(n // tm_mlp, SSM_WIDTH * ncs, CHUNK)
        x2 = _mlp(x2, yret, yst, w_glu[i].astype(BF16), w_out[i].astype(BF16),
                  norm_mix_post[i][None, :], norm_mlp_pre[i][None, :], norm_mlp_post[i][None, :],
                  w_ff1[i].astype(BF16), w_ff2[i].astype(BF16), tm_mlp, ff_chunk)
    return x2.reshape(batch, seq_len, D_MODEL)
```

```python
import functools
import math

import jax
import jax.numpy as jnp
import numpy as np
from jax import lax
from jax.experimental import pallas as pl
from jax.experimental.pallas import tpu as pltpu

D_MODEL = 1024
RET_WIDTH = 512
RET_HEADS = 4
HEAD_DIM = 128
CHUNK = 128
HALF = CHUNK // 2
ROPE_BASE = 10000.0
SSM_WIDTH = 512
SSM_GROUP = 16
SSM_GROUPS = 32
SSM_STATE = 64
D_FF = 4096
NORM_EPS = 1e-6
LANES = 128
MXU_DIM = 256
MLP_ROW_SPLITS = 2

BF16 = jnp.bfloat16
F32 = jnp.float32

VMEM_LIMIT_BYTES = 56 * 1024 * 1024

ROW_BRR, ROW_BII, ROW_BB1, ROW_BB2, ROW_RC1, ROW_RC2, ROW_LAM, ROW_L64, ROW_DSK = 0, 16, 32, 48, 64, 80, 96, 110, 112
MAX_SCAN_LEVELS = (ROW_L64 - ROW_LAM) // 2


def _dot(a, b):
    return jnp.dot(a, b, preferred_element_type=F32)


def _dot_nt(a, b):
    return lax.dot_general(a, b, (((1,), (1,)), ((), ())), preferred_element_type=F32)


def _dot_tn(a, b):
    return lax.dot_general(a, b, (((0,), (0,)), ((), ())), preferred_element_type=F32)


def _rms(x, g):
    ms = jnp.mean(x * x, axis=-1, keepdims=True)
    return x * lax.rsqrt(ms + NORM_EPS) * g


def _mix_in_kernel(x_ref, g_ref, wqvg_ref, wkut_ref, cosf_ref, sinf_ref, cost_ref, sint_ref,
                   xiq_ref, ztt_ref, gch_ref, gn_ref, yret_ref, ut_ref, r_ref):
    @pl.when(pl.program_id(1) == 0)
    def _():
        r_ref[...] = jnp.zeros_like(r_ref)

    tm = x_ref.shape[0]
    half = HEAD_DIM // 2
    h = _rms(x_ref[...], g_ref[...]).astype(BF16)
    p = _dot(h, wqvg_ref[...])
    pt = _dot_nt(wkut_ref[...], h)
    ncs = tm // CHUNK
    for c in range(ncs):
        ut_ref[0, pl.ds(c, SSM_WIDTH, stride=ncs), :] = pt[RET_WIDTH:, c * CHUNK:(c + 1) * CHUNK]

    row = lax.broadcasted_iota(jnp.int32, (CHUNK, CHUNK), 0)
    col = lax.broadcasted_iota(jnp.int32, (CHUNK, CHUNK), 1)
    causal = row >= col
    cosf, sinf = cosf_ref[...], sinf_ref[...]
    cost, sint = cost_ref[...], sint_ref[...]
    for hd in range(RET_HEADS):
        sl = slice(hd * HEAD_DIM, (hd + 1) * HEAD_DIM)
        qh = p[:, sl]
        qh = qh * cosf + pltpu.roll(qh, half, 1) * sinf
        kh = pt[sl, :]
        sw = jnp.concatenate([kh[half:, :], kh[:half, :]], axis=0)
        kh = ((kh * cost + sw * sint) * ztt_ref[hd]).astype(BF16)
        g_state = r_ref[hd]
        for c in range(tm // CHUNK):
            rows = slice(c * CHUNK, (c + 1) * CHUNK)
            qc = (qh[rows, :] * xiq_ref[hd]).astype(BF16)
            kc = kh[:, rows]
            vc = p[rows, RET_WIDTH + hd * HEAD_DIM:RET_WIDTH + (hd + 1) * HEAD_DIM].astype(BF16)
            s = jnp.where(causal, _dot(qc, kc), 0.0).astype(BF16)
            y = _dot(jnp.concatenate([s, qc], axis=1), jnp.concatenate([vc, g_state.astype(BF16)], axis=0))
            g_state = gch_ref[hd] * (g_state + _dot(kc, vc))
            mu = jnp.mean(y, axis=-1, keepdims=True)
            yc = y - mu
            var = jnp.mean(yc * yc, axis=-1, keepdims=True)
            yn = yc * lax.rsqrt(var + NORM_EPS) * gn_ref[hd]
            gate = p[rows, 2 * RET_WIDTH + hd * HEAD_DIM:2 * RET_WIDTH + (hd + 1) * HEAD_DIM]
            yret_ref[rows, sl] = (jax.nn.silu(gate) * yn).astype(BF16)
        r_ref[hd] = g_state


def _mix_in(x2, g_pre, wqvg, wkut, cosf, sinf, cost, sint, xiq, ztt, gch, gn, batch, seq_len, tm):
    n = x2.shape[0]
    nseq = seq_len // tm
    tok = lambda b, j: (b * nseq + j, 0)
    tile3 = lambda b, j: (b * nseq + j, 0, 0)
    const2 = lambda b, j: (0, 0)
    const3 = lambda b, j: (0, 0, 0)
    return pl.pallas_call(
        _mix_in_kernel,
        grid=(batch, nseq),
        in_specs=[
            pl.BlockSpec((tm, D_MODEL), tok),
            pl.BlockSpec((1, D_MODEL), const2),
            pl.BlockSpec((D_MODEL, 3 * RET_WIDTH), const2),
            pl.BlockSpec((RET_WIDTH + SSM_WIDTH, D_MODEL), const2),
            pl.BlockSpec((tm, HEAD_DIM), lambda b, j: (j, 0)),
            pl.BlockSpec((tm, HEAD_DIM), lambda b, j: (j, 0)),
            pl.BlockSpec((HEAD_DIM, tm), lambda b, j: (0, j)),
            pl.BlockSpec((HEAD_DIM, tm), lambda b, j: (0, j)),
            pl.BlockSpec((RET_HEADS, CHUNK, HEAD_DIM), const3),
            pl.BlockSpec((RET_HEADS, 1, tm), const3),
            pl.BlockSpec((RET_HEADS, 1, HEAD_DIM), const3),
            pl.BlockSpec((RET_HEADS, 1, HEAD_DIM), const3),
        ],
        out_specs=[
            pl.BlockSpec((tm, RET_WIDTH), tok),
            pl.BlockSpec((1, SSM_WIDTH * (tm // CHUNK), CHUNK), tile3),
        ],
        out_shape=[
            jax.ShapeDtypeStruct((n, RET_WIDTH), BF16),
            jax.ShapeDtypeStruct((n // tm, SSM_WIDTH * (tm // CHUNK), CHUNK), F32),
        ],
        scratch_shapes=[pltpu.VMEM((RET_HEADS, HEAD_DIM, HEAD_DIM), F32)],
        compiler_params=pltpu.CompilerParams(
            dimension_semantics=("parallel", "arbitrary"), vmem_limit_bytes=VMEM_LIMIT_BYTES),
        name="mix_in",
    )(x2, g_pre, wqvg, wkut, cosf, sinf, cost, sint, xiq, ztt, gch, gn)


def _s5_kernel(u_ref, pw3_ref, sm_ref, y_ref, wtap_ref, wst_ref, wct_ref, ktoe_ref, *, nc, nlev):
    m = u_ref.shape[0] * u_ref.shape[2]
    npair = SSM_GROUP // 2

    def smrow(base, i):
        return sm_ref[0, base + i:base + i + 1, :]

    def lo_half(shape):
        return lax.broadcasted_iota(jnp.int32, shape, 1) < HALF

    def swap(x):
        return pltpu.roll(x, HALF, 1)

    bb1 = sm_ref[0, ROW_BB1:ROW_BB1 + SSM_GROUP, :]
    bb2 = sm_ref[0, ROW_BB2:ROW_BB2 + SSM_GROUP, :]
    mc = lambda co: bb1 * smrow(ROW_RC1, co) + bb2 * smrow(ROW_RC2, co)
    mc_even = jnp.concatenate([mc(2 * cq) for cq in range(npair)], axis=0)
    mc_odd = jnp.concatenate([mc(2 * cq + 1) for cq in range(npair)], axis=0)
    pwk = pw3_ref[0, 0]
    lo_sq = lo_half((2 * SSM_STATE, CHUNK))
    pw_lo = jnp.where(lo_sq, pwk, 0.0)
    pw_hi = jnp.where(lo_sq, 0.0, swap(pwk))
    wtap_ref[...] = (
        jnp.dot(mc_even, pw_lo, precision=lax.Precision.HIGHEST, preferred_element_type=F32)
        + jnp.dot(mc_odd, pw_hi, precision=lax.Precision.HIGHEST, preferred_element_type=F32))

    sgn = jnp.where(lo_half((CHUNK, LANES)), -1.0, 1.0)
    a1 = pw3_ref[0, 2].T
    a2 = swap(a1) * sgn
    a1, a2 = a1[HALF:, :], a2[HALF:, :]
    for ci in range(SSM_GROUP):
        wst_ref[ci * HALF:(ci + 1) * HALF, :] = (
            a1 * smrow(ROW_BRR, ci) + a2 * smrow(ROW_BII, ci)).astype(BF16)
    p1t = pw3_ref[0, 1].T
    b1 = (p1t * (-sgn))[:HALF, :]
    b2 = (-swap(p1t))[:HALF, :]
    for co in range(SSM_GROUP):
        wct_ref[co * HALF:(co + 1) * HALF, :] = (
            b1 * smrow(ROW_RC1, co) + b2 * smrow(ROW_RC2, co)).astype(BF16)

    u = [u_ref[:, ci].reshape(m, CHUNK) for ci in range(SSM_GROUP)]
    lo_m = lo_half((m, LANES))
    a_even = [jnp.where(lo_m, u[2 * cp], swap(u[2 * cp + 1])).astype(BF16) for cp in range(npair)]
    a_odd = [jnp.where(lo_m, swap(u[2 * cp]), u[2 * cp + 1]).astype(BF16) for cp in range(npair)]
    a = jnp.concatenate([jnp.concatenate(a_even, axis=1), jnp.concatenate(a_odd, axis=1)], axis=0)
    st = _dot(a, wst_ref[...])
    s_even, s_odd = st[:m, :], st[m:, :]

    def cmul(x, xs, base, i):
        lr, li = smrow(base, i), smrow(base, i + 1)
        return x * lr + xs * li, xs * lr - x * li

    se_s = swap(s_even)
    st, sts = cmul(s_even, se_s, ROW_L64, 0)
    st, sts = st + s_odd, sts + swap(s_odd)
    block_id = lax.broadcasted_iota(jnp.int32, (m, LANES), 0) % nc
    for lev in range(nlev):
        k = 1 << lev
        keep = block_id >= k
        inc, incs = cmul(jnp.where(keep, pltpu.roll(st, k, 0), 0.0), jnp.where(keep, pltpu.roll(sts, k, 0), 0.0),
                         ROW_LAM, 2 * lev)
        st, sts = st + inc, sts + incs
    first = block_id >= 1
    xprev_even = jnp.where(first, pltpu.roll(st, 1, 0), 0.0)
    xprev_even_s = jnp.where(first, pltpu.roll(sts, 1, 0), 0.0)
    xprev_odd = cmul(xprev_even, xprev_even_s, ROW_L64, 0)[0] + s_even
    xprev = jnp.concatenate([xprev_even, xprev_odd], axis=0).astype(BF16)

    hrow = lax.broadcasted_iota(jnp.int32, (HALF, CHUNK), 0)
    hcol = lax.broadcasted_iota(jnp.int32, (HALF, CHUNK), 1) % HALF
    valid = hcol >= hrow
    nblk = MXU_DIM // CHUNK
    for j in range(npair // nblk):
        for cc in range(nblk):
            cq = j * nblk + cc
            for ci in range(SSM_GROUP):
                tile = jnp.broadcast_to(wtap_ref[cq * SSM_GROUP + ci:cq * SSM_GROUP + ci + 1, :], (HALF, CHUNK))
                tile = pltpu.roll(tile, 0, 1, stride=1, stride_axis=0)
                ktoe_ref[j, ci * HALF:(ci + 1) * HALF, cc * CHUNK:(cc + 1) * CHUNK] = (
                    jnp.where(valid, tile, 0.0).astype(BF16))
        yj = _dot(a, ktoe_ref[j]) + _dot_nt(xprev, wct_ref[j * MXU_DIM:(j + 1) * MXU_DIM, :])
        for cc in range(nblk):
            cq = j * nblk + cc
            y_even = yj[:m, cc * CHUNK:(cc + 1) * CHUNK]
            y_odd = yj[m:, cc * CHUNK:(cc + 1) * CHUNK]
            outs = (jnp.where(lo_m, y_even, swap(y_odd)), jnp.where(lo_m, swap(y_even), y_odd))
            for i in range(2):
                co = 2 * cq + i
                yc = outs[i] + smrow(ROW_DSK, co) * u[co]
                y_ref[:, co] = jax.nn.gelu(yc).reshape(y_ref.shape[0], y_ref.shape[2], CHUNK)


def _s5(ut4, pw3, sm, nc, nlev):
    nt, _, ncs, _ = ut4.shape
    grp = lambda g: (g, 0, 0)
    ugrp = lambda g: (0, g, 0, 0)
    return pl.pallas_call(
        functools.partial(_s5_kernel, nc=nc, nlev=nlev),
        grid=(SSM_GROUPS,),
        in_specs=[
            pl.BlockSpec((nt, SSM_GROUP, ncs, CHUNK), ugrp),
            pl.BlockSpec((1, 3, 2 * SSM_STATE, CHUNK), lambda g: (g, 0, 0, 0)),
            pl.BlockSpec((1, LANES, LANES), grp),
        ],
        out_specs=pl.BlockSpec((nt, SSM_GROUP, ncs, CHUNK), ugrp),
        out_shape=jax.ShapeDtypeStruct(ut4.shape, F32),
        scratch_shapes=[
            pltpu.VMEM((SSM_GROUP * SSM_GROUP // 2, CHUNK), F32),
            pltpu.VMEM((SSM_GROUP * HALF, 2 * SSM_STATE), BF16),
            pltpu.VMEM((SSM_GROUP * HALF, 2 * SSM_STATE), BF16),
            pltpu.VMEM((SSM_GROUP * HALF // MXU_DIM, SSM_GROUP * HALF, MXU_DIM), BF16),
        ],
        compiler_params=pltpu.CompilerParams(
            dimension_semantics=("parallel",), vmem_limit_bytes=VMEM_LIMIT_BYTES),
        name="s5",
    )(ut4, pw3, sm)


def _mlp_kernel(x_ref, yret_ref, yst_ref, wglu_ref, wout_ref, gpost_ref, gpre2_ref, gpost2_ref,
                w1_ref, w2_ref, o_ref, *, ff_chunk, row_splits):
    tm = x_ref.shape[0]
    ncs = tm // CHUNK
    rs, cps = tm // row_splits, ncs // row_splits
    x1s, hs = [], []
    for sp in range(row_splits):
        rows = slice(sp * rs, (sp + 1) * rs)
        yst = jnp.concatenate([yst_ref[0, pl.ds(c, SSM_WIDTH, stride=ncs), :].astype(BF16)
                               for c in range(sp * cps, (sp + 1) * cps)], axis=1)
        glu = _dot_tn(yst, wglu_ref[...])
        yssm = (glu[:, :SSM_WIDTH] * jax.nn.sigmoid(glu[:, SSM_WIDTH:])).astype(BF16)
        mix = _dot(yret_ref[rows, :], wout_ref[:RET_WIDTH, :]) + _dot(yssm, wout_ref[RET_WIDTH:, :])
        x1 = x_ref[rows, :] + _rms(mix, gpost_ref[...])
        x1s.append(x1)
        hs.append(_rms(x1, gpre2_ref[...]).astype(BF16))
    mms = [None] * row_splits
    for c in range(D_FF // ff_chunk):
        for sp in range(row_splits):
            f = jnp.maximum(_dot(hs[sp], w1_ref[:, c * ff_chunk:(c + 1) * ff_chunk]), 0.0)
            part = _dot((f * f).astype(BF16), w2_ref[c * ff_chunk:(c + 1) * ff_chunk, :])
            mms[sp] = part if mms[sp] is None else mms[sp] + part
    for sp in range(row_splits):
        o_ref[sp * rs:(sp + 1) * rs, :] = x1s[sp] + _rms(mms[sp], gpost2_ref[...])


def _mlp(x2, yret, yst, wglu, wout, gpost, gpre2, gpost2, w1, w2, tm, ff_chunk):
    n = x2.shape[0]
    const = lambda i: (0, 0)
    once = pl.Buffered(1)
    return pl.pallas_call(
        functools.partial(_mlp_kernel, ff_chunk=ff_chunk, row_splits=MLP_ROW_SPLITS),
        grid=(n // tm,),
        in_specs=[
            pl.BlockSpec((tm, D_MODEL), lambda i: (i, 0)),
            pl.BlockSpec((tm, RET_WIDTH), lambda i: (i, 0)),
            pl.BlockSpec((1, SSM_WIDTH * (tm // CHUNK), CHUNK), lambda i: (i, 0, 0)),
            pl.BlockSpec((SSM_WIDTH, 2 * SSM_WIDTH), const, pipeline_mode=once),
            pl.BlockSpec((D_MODEL, D_MODEL), const, pipeline_mode=once),
            pl.BlockSpec((1, D_MODEL), const),
            pl.BlockSpec((1, D_MODEL), const),
            pl.BlockSpec((1, D_MODEL), const),
            pl.BlockSpec((D_MODEL, D_FF), const, pipeline_mode=once),
            pl.BlockSpec((D_FF, D_MODEL), const, pipeline_mode=once),
        ],
        out_specs=pl.BlockSpec((tm, D_MODEL), lambda i: (i, 0)),
        out_shape=jax.ShapeDtypeStruct((n, D_MODEL), F32),
        compiler_params=pltpu.CompilerParams(
            dimension_semantics=("parallel",), vmem_limit_bytes=VMEM_LIMIT_BYTES),
        name="mlp",
    )(x2, yret, yst, wglu, wout, gpost, gpre2, gpost2, w1, w2)


def _rope_tables(seq_len):
    half = HEAD_DIM // 2
    inv_freq = ROPE_BASE ** (-np.arange(half, dtype=np.float64) / half)
    ang = np.arange(seq_len, dtype=np.float64)[:, None] * inv_freq[None, :]
    cos, sin = np.cos(ang), np.sin(ang)
    cosf = np.concatenate([cos, cos], axis=1)
    sinf = np.concatenate([-sin, sin], axis=1)
    kscale = HEAD_DIM ** -0.5
    f32 = lambda t: jnp.asarray(np.ascontiguousarray(t), dtype=F32)
    return f32(cosf), f32(sinf), f32(cosf.T * kscale), f32(sinf.T * kscale)


def _retention_tables(tm):
    log_gamma = np.log(1.0 - np.exp(np.linspace(math.log(1.0 / 32), math.log(1.0 / 512), RET_HEADS)))
    idx = np.arange(CHUNK, dtype=np.float64)
    xiq = np.exp((idx + 1.0 - CHUNK)[None, :] * log_gamma[:, None])
    zeta = np.exp((CHUNK - 1 - idx)[None, :] * log_gamma[:, None])
    gch = np.exp(CHUNK * log_gamma)
    xiq_b = np.broadcast_to(xiq[:, :, None], (RET_HEADS, CHUNK, HEAD_DIM))
    ztt = np.tile(zeta, (1, tm // CHUNK))[:, None, :]
    gch_b = np.broadcast_to(gch[:, None, None], (RET_HEADS, 1, HEAD_DIM))
    f32 = lambda t: jnp.asarray(np.ascontiguousarray(t), dtype=F32)
    return f32(xiq_b), f32(ztt), f32(gch_b)


def _s5_tables(lam_re, lam_im, log_dt, b_re, b_im, c_re, c_im, d_skip, nlev):
    assert nlev <= MAX_SCAN_LEVELS
    a = jnp.minimum(lam_re.astype(F32), -1e-4)
    b = lam_im.astype(F32)
    dt = jnp.exp(log_dt.astype(F32))[:, None]
    tau = jnp.arange(CHUNK + 1, dtype=F32)
    mag = jnp.exp((a * dt)[:, :, None] * tau)
    ph = (b * dt)[:, :, None] * tau
    pr, pi = mag * jnp.cos(ph), mag * jnp.sin(ph)
    lr1, li = pr[:, :, 1] - 1.0, pi[:, :, 1]
    den = a * a + b * b
    c0r, c0i = (lr1 * a + li * b) / den, (li * a - lr1 * b) / den
    bre, bim = b_re.astype(F32), b_im.astype(F32)
    br = jnp.swapaxes(c0r[:, :, None] * bre - c0i[:, :, None] * bim, 1, 2)
    bi = jnp.swapaxes(c0r[:, :, None] * bim + c0i[:, :, None] * bre, 1, 2)
    cr, ci = c_re.astype(F32), c_im.astype(F32)

    pall = jnp.concatenate([pr, pi], axis=1)
    pw3 = jnp.stack([pall[:, :, :CHUNK], pall[:, :, 1:], pall[:, :, CHUNK - 1::-1]], axis=1)
    cat = lambda x, y: jnp.concatenate([x, y], axis=-1)

    lam_rows = []
    for step in [float(CHUNK * (1 << lev)) for lev in range(MAX_SCAN_LEVELS)] + [float(HALF)]:
        mg = jnp.exp(a * dt * step)
        re, im = mg * jnp.cos(b * dt * step), mg * jnp.sin(b * dt * step)
        lam_rows += [cat(re, re), cat(-im, im)]
    lamtab = jnp.stack(lam_rows, axis=1)
    dsk = jnp.broadcast_to(d_skip.astype(F32).reshape(SSM_GROUPS, SSM_GROUP, 1), (SSM_GROUPS, SSM_GROUP, LANES))
    sm = jnp.concatenate([cat(br, br), cat(bi, bi), cat(br, -bi), cat(-bi, -br), cat(cr, cr), cat(ci, ci),
                          lamtab, dsk], axis=1)
    return pw3, sm


def _tiles(seq_len):
    tm_mix = min(1024, seq_len)
    tm_mlp = tm_mix
    ff_chunk = 1024
    assert seq_len % CHUNK == 0 and seq_len % tm_mix == 0
    return tm_mix, tm_mlp, ff_chunk


def kernel(x, norm_mix_pre, norm_mix_post, w_in, ret_gn_gain, ssm_lambda_re, ssm_lambda_im, ssm_log_dt,
           ssm_b_re, ssm_b_im, ssm_c_re, ssm_c_im, ssm_d, w_glu, w_out, norm_mlp_pre, norm_mlp_post,
           w_ff1, w_ff2):
    batch, seq_len, _ = x.shape
    depth = w_in.shape[0]
    n = batch * seq_len
    nc = seq_len // CHUNK
    nlev = max(1, (nc - 1).bit_length())
    tm_mix, tm_mlp, ff_chunk = _tiles(seq_len)

    cosf, sinf, cost, sint = _rope_tables(seq_len)
    xiq, ztt, gch = _retention_tables(tm_mix)
    x2 = x.reshape(n, D_MODEL)
    for i in range(depth):
        wi = w_in[i]
        wqvg = jnp.concatenate([wi[:, :RET_WIDTH], wi[:, 2 * RET_WIDTH:4 * RET_WIDTH]], axis=1).astype(BF16)
        wkut = jnp.concatenate([wi[:, RET_WIDTH:2 * RET_WIDTH], wi[:, 4 * RET_WIDTH:]], axis=1).T.astype(BF16)
        gn = ret_gn_gain[i].astype(F32).reshape(RET_HEADS, 1, HEAD_DIM)
        yret, ut = _mix_in(x2, norm_mix_pre[i][None, :], wqvg, wkut, cosf, sinf, cost, sint,
                           xiq, ztt, gch, gn, batch, seq_len, tm_mix)
        pw3, sm = _s5_tables(
            ssm_lambda_re[i], ssm_lambda_im[i], ssm_log_dt[i], ssm_b_re[i], ssm_b_im[i],
            ssm_c_re[i], ssm_c_im[i], ssm_d[i], nlev)
        ncs = tm_mix // CHUNK
        yst = _s5(ut.reshape(n // tm_mix, SSM_WIDTH, ncs, CHUNK), pw3, sm, nc, nlev)
        yst = yst.reshape(n // tm_mlp, SSM_WIDTH * ncs, CHUNK)
        x2 = _mlp(x2, yret, yst, w_glu[i].astype(BF16), w_out[i].astype(BF16),
                  norm_mix_post[i][None, :], norm_mlp_pre[i][None, :], norm_mlp_post[i][None, :],
                  w_ff1[i].astype(BF16), w_ff2[i].astype(BF16), tm_mlp, ff_chunk)
    return x2.reshape(batch, seq_len, D_MODEL)
```

```python
import functools
import math

import jax
import jax.numpy as jnp
import numpy as np
from jax import lax
from jax.experimental import pallas as pl
from jax.experimental.pallas import tpu as pltpu

D_MODEL = 1024
RET_WIDTH = 512
RET_HEADS = 4
HEAD_DIM = 128
CHUNK = 128
HALF = CHUNK // 2
ROPE_BASE = 10000.0
SSM_WIDTH = 512
SSM_GROUP = 16
SSM_GROUPS = 32
SSM_STATE = 64
D_FF = 4096
NORM_EPS = 1e-6
LANES = 128
MXU_DIM = 256
MLP_ROW_SPLITS = 2

BF16 = jnp.bfloat16
F32 = jnp.float32

VMEM_LIMIT_BYTES = 56 * 1024 * 1024

ROW_BRR, ROW_BII, ROW_BB1, ROW_BB2, ROW_RC1, ROW_RC2, SM_ROWS = 0, 16, 32, 48, 64, 80, 96
TAB_LAM, TAB_L64, TAB_DSK, TAB_ROWS = 0, 14, 16, 32
MAX_SCAN_LEVELS = (TAB_L64 - TAB_LAM) // 2


def _dot(a, b):
    return jnp.dot(a, b, preferred_element_type=F32)


def _dot_nt(a, b):
    return lax.dot_general(a, b, (((1,), (1,)), ((), ())), preferred_element_type=F32)


def _dot_tn(a, b):
    return lax.dot_general(a, b, (((0,), (0,)), ((), ())), preferred_element_type=F32)


def _rms(x, g):
    ms = jnp.mean(x * x, axis=-1, keepdims=True)
    return x * lax.rsqrt(ms + NORM_EPS) * g


def _mix_in_kernel(x_ref, g_ref, win_ref, wkut_ref, cosf_ref, sinf_ref, cost_ref, sint_ref,
                   xiq_ref, ztt_ref, gch_ref, gn_ref, yret_ref, ut_ref, r_ref):
    @pl.when(pl.program_id(1) == 0)
    def _():
        r_ref[...] = jnp.zeros_like(r_ref)

    tm = x_ref.shape[0]
    half = HEAD_DIM // 2
    h = _rms(x_ref[...], g_ref[...]).astype(BF16)
    pq = _dot(h, win_ref[:, :RET_WIDTH])
    pvg = _dot(h, win_ref[:, 2 * RET_WIDTH:4 * RET_WIDTH])
    pt = _dot_nt(wkut_ref[...], h)
    ncs = tm // CHUNK
    for c in range(ncs):
        ut_ref[0, pl.ds(c, SSM_WIDTH, stride=ncs), :] = pt[RET_WIDTH:, c * CHUNK:(c + 1) * CHUNK]

    row = lax.broadcasted_iota(jnp.int32, (CHUNK, CHUNK), 0)
    col = lax.broadcasted_iota(jnp.int32, (CHUNK, CHUNK), 1)
    causal = row >= col
    cosf, sinf = cosf_ref[...], sinf_ref[...]
    cost, sint = cost_ref[...], sint_ref[...]
    for hd in range(RET_HEADS):
        sl = slice(hd * HEAD_DIM, (hd + 1) * HEAD_DIM)
        qh = pq[:, sl]
        qh = qh * cosf + pltpu.roll(qh, half, 1) * sinf
        kh = pt[sl, :]
        sw = jnp.concatenate([kh[half:, :], kh[:half, :]], axis=0)
        kh = ((kh * cost + sw * sint) * ztt_ref[hd]).astype(BF16)
        g_state = r_ref[hd]
        for c in range(tm // CHUNK):
            rows = slice(c * CHUNK, (c + 1) * CHUNK)
            qc = (qh[rows, :] * xiq_ref[hd]).astype(BF16)
            kc = kh[:, rows]
            vc = pvg[rows, sl].astype(BF16)
            s = jnp.where(causal, _dot(qc, kc), 0.0).astype(BF16)
            y = _dot(jnp.concatenate([s, qc], axis=1), jnp.concatenate([vc, g_state.astype(BF16)], axis=0))
            g_state = gch_ref[hd] * (g_state + _dot(kc, vc))
            mu = jnp.mean(y, axis=-1, keepdims=True)
            yc = y - mu
            var = jnp.mean(yc * yc, axis=-1, keepdims=True)
            yn = yc * lax.rsqrt(var + NORM_EPS) * gn_ref[hd]
            gate = pvg[rows, RET_WIDTH + hd * HEAD_DIM:RET_WIDTH + (hd + 1) * HEAD_DIM]
            yret_ref[rows, sl] = (jax.nn.silu(gate) * yn).astype(BF16)
        r_ref[hd] = g_state


def _mix_in(x2, g_pre, win, wkut, cosf, sinf, cost, sint, xiq, ztt, gch, gn, batch, seq_len, tm):
    n = x2.shape[0]
    nseq = seq_len // tm
    tok = lambda b, j: (b * nseq + j, 0)
    tile3 = lambda b, j: (b * nseq + j, 0, 0)
    const2 = lambda b, j: (0, 0)
    const3 = lambda b, j: (0, 0, 0)
    return pl.pallas_call(
        _mix_in_kernel,
        grid=(batch, nseq),
        in_specs=[
            pl.BlockSpec((tm, D_MODEL), tok),
            pl.BlockSpec((1, D_MODEL), const2),
            pl.BlockSpec((D_MODEL, 4 * RET_WIDTH + SSM_WIDTH), const2),
            pl.BlockSpec((RET_WIDTH + SSM_WIDTH, D_MODEL), const2),
            pl.BlockSpec((tm, HEAD_DIM), lambda b, j: (j, 0)),
            pl.BlockSpec((tm, HEAD_DIM), lambda b, j: (j, 0)),
            pl.BlockSpec((HEAD_DIM, tm), lambda b, j: (0, j)),
            pl.BlockSpec((HEAD_DIM, tm), lambda b, j: (0, j)),
            pl.BlockSpec((RET_HEADS, CHUNK, HEAD_DIM), const3),
            pl.BlockSpec((RET_HEADS, 1, tm), const3),
            pl.BlockSpec((RET_HEADS, 1, HEAD_DIM), const3),
            pl.BlockSpec((RET_HEADS, 1, HEAD_DIM), const3),
        ],
        out_specs=[
            pl.BlockSpec((tm, RET_WIDTH), tok),
            pl.BlockSpec((1, SSM_WIDTH * (tm // CHUNK), CHUNK), tile3),
        ],
        out_shape=[
            jax.ShapeDtypeStruct((n, RET_WIDTH), BF16),
            jax.ShapeDtypeStruct((n // tm, SSM_WIDTH * (tm // CHUNK), CHUNK), F32),
        ],
        scratch_shapes=[pltpu.VMEM((RET_HEADS, HEAD_DIM, HEAD_DIM), F32)],
        compiler_params=pltpu.CompilerParams(
            dimension_semantics=("parallel", "arbitrary"), vmem_limit_bytes=VMEM_LIMIT_BYTES),
        name="mix_in",
    )(x2, g_pre, win, wkut, cosf, sinf, cost, sint, xiq, ztt, gch, gn)


def _s5_kernel(u_ref, pwk_ref, bc_ref, tab_ref, y_ref, sm_ref, wtap_ref, wst_ref, wct_ref, ktoe_ref, *, nc, nlev):
    m = u_ref.shape[0] * u_ref.shape[2]
    npair = SSM_GROUP // 2

    def smrow(base, i):
        return sm_ref[base + i:base + i + 1, :]

    def tabrow(base, i):
        return tab_ref[0, base + i:base + i + 1, :]

    def lo_half(shape):
        return lax.broadcasted_iota(jnp.int32, shape, 1) < HALF

    def swap(x):
        return pltpu.roll(x, HALF, 1)

    br0, bi0, cr0, ci0 = (bc_ref[0, k * SSM_GROUP:(k + 1) * SSM_GROUP, :] for k in range(4))
    sm_ref[ROW_BRR:ROW_BRR + SSM_GROUP, :] = br0 + swap(br0)
    sm_ref[ROW_BII:ROW_BII + SSM_GROUP, :] = bi0 + swap(bi0)
    sm_ref[ROW_BB1:ROW_BB1 + SSM_GROUP, :] = br0 - swap(bi0)
    sm_ref[ROW_BB2:ROW_BB2 + SSM_GROUP, :] = -bi0 - swap(br0)
    sm_ref[ROW_RC1:ROW_RC1 + SSM_GROUP, :] = cr0 + swap(cr0)
    sm_ref[ROW_RC2:ROW_RC2 + SSM_GROUP, :] = ci0 + swap(ci0)

    bb1 = sm_ref[ROW_BB1:ROW_BB1 + SSM_GROUP, :]
    bb2 = sm_ref[ROW_BB2:ROW_BB2 + SSM_GROUP, :]
    mc = lambda co: bb1 * smrow(ROW_RC1, co) + bb2 * smrow(ROW_RC2, co)
    mc_even = jnp.concatenate([mc(2 * cq) for cq in range(npair)], axis=0)
    mc_odd = jnp.concatenate([mc(2 * cq + 1) for cq in range(npair)], axis=0)
    pwk = pwk_ref[0]
    lo_sq = lo_half((2 * SSM_STATE, CHUNK))
    pw_lo = jnp.where(lo_sq, pwk, 0.0)
    pw_hi = jnp.where(lo_sq, 0.0, swap(pwk))
    wtap_ref[...] = (
        jnp.dot(mc_even, pw_lo, precision=lax.Precision.HIGHEST, preferred_element_type=F32)
        + jnp.dot(mc_odd, pw_hi, precision=lax.Precision.HIGHEST, preferred_element_type=F32))

    sgn = jnp.where(lo_half((HALF, LANES)), -1.0, 1.0)
    pwt = pwk.T[:HALF, :]
    rev = (lax.broadcasted_iota(jnp.int32, (HALF, HALF), 0)
           + lax.broadcasted_iota(jnp.int32, (HALF, HALF), 1)) == HALF - 1
    a1 = jnp.dot(jnp.where(rev, 1.0, 0.0), pwt, precision=lax.Precision.HIGHEST,
                 preferred_element_type=F32)
    a2 = swap(a1) * sgn
    for ci in range(SSM_GROUP):
        wst_ref[ci * HALF:(ci + 1) * HALF, :] = (
            a1 * smrow(ROW_BRR, ci) + a2 * smrow(ROW_BII, ci)).astype(BF16)
    lam_row = pwt[1:2, :]
    lo_row = lo_half((1, LANES))
    lam_r = jnp.where(lo_row, lam_row, swap(lam_row))
    lam_i = jnp.where(lo_row, -swap(lam_row), lam_row)
    p1t = pwt * lam_r + swap(pwt) * lam_i
    b1 = p1t * (-sgn)
    b2 = -swap(p1t)
    for co in range(SSM_GROUP):
        wct_ref[co * HALF:(co + 1) * HALF, :] = (
            b1 * smrow(ROW_RC1, co) + b2 * smrow(ROW_RC2, co)).astype(BF16)

    u = [u_ref[:, ci].reshape(m, CHUNK) for ci in range(SSM_GROUP)]
    lo_m = lo_half((m, LANES))
    a_even = [jnp.where(lo_m, u[2 * cp], swap(u[2 * cp + 1])).astype(BF16) for cp in range(npair)]
    a_odd = [jnp.where(lo_m, swap(u[2 * cp]), u[2 * cp + 1]).astype(BF16) for cp in range(npair)]
    a = jnp.concatenate([jnp.concatenate(a_even, axis=1), jnp.concatenate(a_odd, axis=1)], axis=0)
    st = _dot(a, wst_ref[...])
    s_even, s_odd = st[:m, :], st[m:, :]

    def cmul(x, xs, base, i):
        lr, li = tabrow(base, i), tabrow(base, i + 1)
        return x * lr + xs * li, xs * lr - x * li

    se_s = swap(s_even)
    st, sts = cmul(s_even, se_s, TAB_L64, 0)
    st, sts = st + s_odd, sts + swap(s_odd)
    block_id = lax.broadcasted_iota(jnp.int32, (m, LANES), 0) % nc
    for lev in range(nlev):
        k = 1 << lev
        keep = block_id >= k
        inc, incs = cmul(jnp.where(keep, pltpu.roll(st, k, 0), 0.0), jnp.where(keep, pltpu.roll(sts, k, 0), 0.0),
                         TAB_LAM, 2 * lev)
        st, sts = st + inc, sts + incs
    first = block_id >= 1
    xprev_even = jnp.where(first, pltpu.roll(st, 1, 0), 0.0)
    xprev_even_s = jnp.where(first, pltpu.roll(sts, 1, 0), 0.0)
    xprev_odd = cmul(xprev_even, xprev_even_s, TAB_L64, 0)[0] + s_even
    xprev = jnp.concatenate([xprev_even, xprev_odd], axis=0).astype(BF16)

    hrow = lax.broadcasted_iota(jnp.int32, (HALF, CHUNK), 0)
    hcol = lax.broadcasted_iota(jnp.int32, (HALF, CHUNK), 1) % HALF
    valid = hcol >= hrow
    nblk = MXU_DIM // CHUNK
    for j in range(npair // nblk):
        for cc in range(nblk):
            cq = j * nblk + cc
            for ci in range(SSM_GROUP):
                tile = jnp.broadcast_to(wtap_ref[cq * SSM_GROUP + ci:cq * SSM_GROUP + ci + 1, :], (HALF, CHUNK))
                tile = pltpu.roll(tile, 0, 1, stride=1, stride_axis=0)
                ktoe_ref[j, ci * HALF:(ci + 1) * HALF, cc * CHUNK:(cc + 1) * CHUNK] = (
                    jnp.where(valid, tile, 0.0).astype(BF16))
        yj = _dot(a, ktoe_ref[j]) + _dot_nt(xprev, wct_ref[j * MXU_DIM:(j + 1) * MXU_DIM, :])
        for cc in range(nblk):
            cq = j * nblk + cc
            y_even = yj[:m, cc * CHUNK:(cc + 1) * CHUNK]
            y_odd = yj[m:, cc * CHUNK:(cc + 1) * CHUNK]
            outs = (jnp.where(lo_m, y_even, swap(y_odd)), jnp.where(lo_m, swap(y_even), y_odd))
            for i in range(2):
                co = 2 * cq + i
                yc = outs[i] + tabrow(TAB_DSK, co) * u[co]
                y_ref[:, co] = jax.nn.gelu(yc).reshape(y_ref.shape[0], y_ref.shape[2], CHUNK)


def _s5(ut4, pwk, bc, tab, nc, nlev):
    nt, _, ncs, _ = ut4.shape
    grp = lambda g: (g, 0, 0)
    ugrp = lambda g: (0, g, 0, 0)
    return pl.pallas_call(
        functools.partial(_s5_kernel, nc=nc, nlev=nlev),
        grid=(SSM_GROUPS,),
        in_specs=[
            pl.BlockSpec((nt, SSM_GROUP, ncs, CHUNK), ugrp),
            pl.BlockSpec((1, 2 * SSM_STATE, CHUNK), grp),
            pl.BlockSpec((1, 4 * SSM_GROUP, LANES), grp),
            pl.BlockSpec((1, TAB_ROWS, LANES), grp),
        ],
        out_specs=pl.BlockSpec((nt, SSM_GROUP, ncs, CHUNK), ugrp),
        out_shape=jax.ShapeDtypeStruct(ut4.shape, F32),
        scratch_shapes=[
            pltpu.VMEM((SM_ROWS, LANES), F32),
            pltpu.VMEM((SSM_GROUP * SSM_GROUP // 2, CHUNK), F32),
            pltpu.VMEM((SSM_GROUP * HALF, 2 * SSM_STATE), BF16),
            pltpu.VMEM((SSM_GROUP * HALF, 2 * SSM_STATE), BF16),
            pltpu.VMEM((SSM_GROUP * HALF // MXU_DIM, SSM_GROUP * HALF, MXU_DIM), BF16),
        ],
        compiler_params=pltpu.CompilerParams(
            dimension_semantics=("parallel",), vmem_limit_bytes=VMEM_LIMIT_BYTES),
        name="s5",
    )(ut4, pwk, bc, tab)


def _mlp_kernel(x_ref, yret_ref, yst_ref, wglu_ref, wout_ref, gpost_ref, gpre2_ref, gpost2_ref,
                w1_ref, w2_ref, o_ref, *, ff_chunk, row_splits):
    tm = x_ref.shape[0]
    ncs = tm // CHUNK
    rs, cps = tm // row_splits, ncs // row_splits
    x1s, hs = [], []
    for sp in range(row_splits):
        rows = slice(sp * rs, (sp + 1) * rs)
        yst = jnp.concatenate([yst_ref[0, pl.ds(c, SSM_WIDTH, stride=ncs), :].astype(BF16)
                               for c in range(sp * cps, (sp + 1) * cps)], axis=1)
        glu = _dot_tn(yst, wglu_ref[...])
        yssm = (glu[:, :SSM_WIDTH] * jax.nn.sigmoid(glu[:, SSM_WIDTH:])).astype(BF16)
        mix = _dot(yret_ref[rows, :], wout_ref[:RET_WIDTH, :]) + _dot(yssm, wout_ref[RET_WIDTH:, :])
        x1 = x_ref[rows, :] + _rms(mix, gpost_ref[...])
        x1s.append(x1)
        hs.append(_rms(x1, gpre2_ref[...]).astype(BF16))
    mms = [None] * row_splits
    for c in range(D_FF // ff_chunk):
        for sp in range(row_splits):
            f = jnp.maximum(_dot(hs[sp], w1_ref[:, c * ff_chunk:(c + 1) * ff_chunk]), 0.0)
            part = _dot((f * f).astype(BF16), w2_ref[c * ff_chunk:(c + 1) * ff_chunk, :])
            mms[sp] = part if mms[sp] is None else mms[sp] + part
    for sp in range(row_splits):
        o_ref[sp * rs:(sp + 1) * rs, :] = x1s[sp] + _rms(mms[sp], gpost2_ref[...])


def _mlp(x2, yret, yst, wglu, wout, gpost, gpre2, gpost2, w1, w2, tm, ff_chunk):
    n = x2.shape[0]
    const = lambda i: (0, 0)
    once = pl.Buffered(1)
    return pl.pallas_call(
        functools.partial(_mlp_kernel, ff_chunk=ff_chunk, row_splits=MLP_ROW_SPLITS),
        grid=(n // tm,),
        in_specs=[
            pl.BlockSpec((tm, D_MODEL), lambda i: (i, 0)),
            pl.BlockSpec((tm, RET_WIDTH), lambda i: (i, 0)),
            pl.BlockSpec((1, SSM_WIDTH * (tm // CHUNK), CHUNK), lambda i: (i, 0, 0)),
            pl.BlockSpec((SSM_WIDTH, 2 * SSM_WIDTH), const, pipeline_mode=once),
            pl.BlockSpec((D_MODEL, D_MODEL), const, pipeline_mode=once),
            pl.BlockSpec((1, D_MODEL), const),
            pl.BlockSpec((1, D_MODEL), const),
            pl.BlockSpec((1, D_MODEL), const),
            pl.BlockSpec((D_MODEL, D_FF), const, pipeline_mode=once),
            pl.BlockSpec((D_FF, D_MODEL), const, pipeline_mode=once),
        ],
        out_specs=pl.BlockSpec((tm, D_MODEL), lambda i: (i, 0)),
        out_shape=jax.ShapeDtypeStruct((n, D_MODEL), F32),
        compiler_params=pltpu.CompilerParams(
            dimension_semantics=("parallel",), vmem_limit_bytes=VMEM_LIMIT_BYTES),
        name="mlp",
    )(x2, yret, yst, wglu, wout, gpost, gpre2, gpost2, w1, w2)


def _rope_tables(seq_len):
    half = HEAD_DIM // 2
    inv_freq = ROPE_BASE ** (-np.arange(half, dtype=np.float64) / half)
    ang = np.arange(seq_len, dtype=np.float64)[:, None] * inv_freq[None, :]
    cos, sin = np.cos(ang), np.sin(ang)
    cosf = np.concatenate([cos, cos], axis=1)
    sinf = np.concatenate([-sin, sin], axis=1)
    kscale = HEAD_DIM ** -0.5
    f32 = lambda t: jnp.asarray(np.ascontiguousarray(t), dtype=F32)
    return f32(cosf), f32(sinf), f32(cosf.T * kscale), f32(sinf.T * kscale)


def _retention_tables(tm):
    log_gamma = np.log(1.0 - np.exp(np.linspace(math.log(1.0 / 32), math.log(1.0 / 512), RET_HEADS)))
    idx = np.arange(CHUNK, dtype=np.float64)
    xiq = np.exp((idx + 1.0 - CHUNK)[None, :] * log_gamma[:, None])
    zeta = np.exp((CHUNK - 1 - idx)[None, :] * log_gamma[:, None])
    gch = np.exp(CHUNK * log_gamma)
    xiq_b = np.broadcast_to(xiq[:, :, None], (RET_HEADS, CHUNK, HEAD_DIM))
    ztt = np.tile(zeta, (1, tm // CHUNK))[:, None, :]
    gch_b = np.broadcast_to(gch[:, None, None], (RET_HEADS, 1, HEAD_DIM))
    f32 = lambda t: jnp.asarray(np.ascontiguousarray(t), dtype=F32)
    return f32(xiq_b), f32(ztt), f32(gch_b)


def _s5_tables(lam_re, lam_im, log_dt, b_re, b_im, c_re, c_im, d_skip, nlev):
    assert nlev <= MAX_SCAN_LEVELS
    a = jnp.minimum(lam_re.astype(F32), -1e-4)
    b = lam_im.astype(F32)
    dt = jnp.exp(log_dt.astype(F32))[:, None]
    tau = jnp.arange(CHUNK, dtype=F32)
    mag = jnp.exp((a * dt)[:, :, None] * tau)
    ph = (b * dt)[:, :, None] * tau
    pr, pi = mag * jnp.cos(ph), mag * jnp.sin(ph)
    lr1, li = pr[:, :, 1] - 1.0, pi[:, :, 1]
    den = a * a + b * b
    c0r, c0i = (lr1 * a + li * b) / den, (li * a - lr1 * b) / den
    bre, bim = b_re.astype(F32), b_im.astype(F32)
    br = jnp.swapaxes(c0r[:, :, None] * bre - c0i[:, :, None] * bim, 1, 2)
    bi = jnp.swapaxes(c0r[:, :, None] * bim + c0i[:, :, None] * bre, 1, 2)
    cr, ci = c_re.astype(F32), c_im.astype(F32)

    pwk = jnp.concatenate([pr, pi], axis=1)
    cat = lambda x, y: jnp.concatenate([x, y], axis=-1)

    lam_rows = []
    for step in [float(CHUNK * (1 << lev)) for lev in range(MAX_SCAN_LEVELS)] + [float(HALF)]:
        mg = jnp.exp(a * dt * step)
        re, im = mg * jnp.cos(b * dt * step), mg * jnp.sin(b * dt * step)
        lam_rows += [cat(re, re), cat(-im, im)]
    dsk = jnp.broadcast_to(d_skip.astype(F32).reshape(SSM_GROUPS, SSM_GROUP, 1), (SSM_GROUPS, SSM_GROUP, LANES))
    tab = jnp.concatenate([jnp.stack(lam_rows, axis=1), dsk], axis=1)
    bc = jnp.concatenate([br, bi, cr, ci], axis=1)
    bc = jnp.pad(bc, ((0, 0), (0, 0), (0, LANES - SSM_STATE)))
    return pwk, bc, tab


def _tiles(seq_len):
    tm_mix = min(1024, seq_len)
    tm_mlp = tm_mix
    ff_chunk = 1024
    assert seq_len % CHUNK == 0 and seq_len % tm_mix == 0
    return tm_mix, tm_mlp, ff_chunk


def kernel(x, norm_mix_pre, norm_mix_post, w_in, ret_gn_gain, ssm_lambda_re, ssm_lambda_im, ssm_log_dt,
           ssm_b_re, ssm_b_im, ssm_c_re, ssm_c_im, ssm_d, w_glu, w_out, norm_mlp_pre, norm_mlp_post,
           w_ff1, w_ff2):
    batch, seq_len, _ = x.shape
    depth = w_in.shape[0]
    n = batch * seq_len
    nc = seq_len // CHUNK
    nlev = max(1, (nc - 1).bit_length())
    tm_mix, tm_mlp, ff_chunk = _tiles(seq_len)

    cosf, sinf, cost, sint = _rope_tables(seq_len)
    xiq, ztt, gch = _retention_tables(tm_mix)
    x2 = x.reshape(n, D_MODEL)
    for i in range(depth):
        win = w_in[i].astype(BF16)
        wkut = jnp.concatenate([win[:, RET_WIDTH:2 * RET_WIDTH], win[:, 4 * RET_WIDTH:]], axis=1).T
        gn = ret_gn_gain[i].astype(F32).reshape(RET_HEADS, 1, HEAD_DIM)
        yret, ut = _mix_in(x2, norm_mix_pre[i][None, :], win, wkut, cosf, sinf, cost, sint,
                           xiq, ztt, gch, gn, batch, seq_len, tm_mix)
        pwk, bc, tab = _s5_tables(
            ssm_lambda_re[i], ssm_lambda_im[i], ssm_log_dt[i], ssm_b_re[i], ssm_b_im[i],
            ssm_c_re[i], ssm_c_im[i], ssm_d[i], nlev)
        ncs = tm_mix // CHUNK
        yst = _s5(ut.reshape(n // tm_mix, SSM_WIDTH, ncs, CHUNK), pwk, bc, tab, nc, nlev)
        yst = yst.reshape(n // tm_mlp, SSM_WIDTH * ncs, CHUNK)
        x2 = _mlp(x2, yret, yst, w_glu[i].astype(BF16), w_out[i].astype(BF16),
                  norm_mix_post[i][None, :], norm_mlp_pre[i][None, :], norm_mlp_post[i][None, :],
                  w_ff1[i].astype(BF16), w_ff2[i].astype(BF16), tm_mlp, ff_chunk)
    return x2.reshape(batch, seq_len, D_MODEL)
```

```python
import functools
import math

import jax
import jax.numpy as jnp
import numpy as np
from jax import lax
from jax.experimental import pallas as pl
from jax.experimental.pallas import tpu as pltpu

D_MODEL = 1024
RET_WIDTH = 512
RET_HEADS = 4
HEAD_DIM = 128
CHUNK = 128
HALF = CHUNK // 2
ROPE_BASE = 10000.0
SSM_WIDTH = 512
SSM_GROUP = 16
SSM_GROUPS = 32
SSM_STATE = 64
D_FF = 4096
NORM_EPS = 1e-6
LANES = 128
MXU_DIM = 256
MLP_ROW_SPLITS = 2

BF16 = jnp.bfloat16
F32 = jnp.float32

VMEM_LIMIT_BYTES = 56 * 1024 * 1024

ROW_BRR, ROW_BII, ROW_BB1, ROW_BB2, ROW_RC1, ROW_RC2, SM_ROWS = 0, 16, 32, 48, 64, 80, 96
TAB_LAM, TAB_L64, TAB_DSK, TAB_ROWS = 0, 14, 16, 32
MAX_SCAN_LEVELS = (TAB_L64 - TAB_LAM) // 2


def _dot(a, b):
    return jnp.dot(a, b, preferred_element_type=F32)


def _dot_nt(a, b):
    return lax.dot_general(a, b, (((1,), (1,)), ((), ())), preferred_element_type=F32)


def _dot_tn(a, b):
    return lax.dot_general(a, b, (((0,), (0,)), ((), ())), preferred_element_type=F32)


def _dot_3pass(a, b):
    a_hi = a.astype(BF16)
    a_lo = (a - a_hi.astype(F32)).astype(BF16)
    b_hi = b.astype(BF16)
    b_lo = (b - b_hi.astype(F32)).astype(BF16)
    return _dot(a_hi, b_hi) + (_dot(a_hi, b_lo) + _dot(a_lo, b_hi))


def _rms(x, g):
    ms = jnp.mean(x * x, axis=-1, keepdims=True)
    return x * lax.rsqrt(ms + NORM_EPS) * g


def _mix_in_kernel(x_ref, g_ref, win_ref, cosf_ref, sinf_ref, cost_ref, sint_ref,
                   xiq_ref, ztt_ref, gch_ref, gn_ref, yret_ref, ut_ref, r_ref, wkut_ref):
    @pl.when((pl.program_id(0) == 0) & (pl.program_id(1) == 0))
    def _():
        for blk in range((RET_WIDTH + SSM_WIDTH) // LANES):
            c0 = RET_WIDTH + blk * LANES if blk * LANES < RET_WIDTH else 3 * RET_WIDTH + blk * LANES
            wkut_ref[blk * LANES:(blk + 1) * LANES, :] = (
                win_ref[:, c0:c0 + LANES].astype(F32).T.astype(BF16))

    @pl.when(pl.program_id(1) == 0)
    def _():
        r_ref[...] = jnp.zeros_like(r_ref)

    tm = x_ref.shape[0]
    half = HEAD_DIM // 2
    h = _rms(x_ref[...], g_ref[...]).astype(BF16)
    pq = _dot(h, win_ref[:, :RET_WIDTH])
    pvg = _dot(h, win_ref[:, 2 * RET_WIDTH:4 * RET_WIDTH])
    pt = _dot_nt(wkut_ref[...], h)
    ncs = tm // CHUNK
    for c in range(ncs):
        ut_ref[0, pl.ds(c, SSM_WIDTH, stride=ncs), :] = pt[RET_WIDTH:, c * CHUNK:(c + 1) * CHUNK]

    row = lax.broadcasted_iota(jnp.int32, (CHUNK, CHUNK), 0)
    col = lax.broadcasted_iota(jnp.int32, (CHUNK, CHUNK), 1)
    causal = row >= col
    cosf, sinf = cosf_ref[...], sinf_ref[...]
    cost, sint = cost_ref[...], sint_ref[...]
    for hd in range(RET_HEADS):
        sl = slice(hd * HEAD_DIM, (hd + 1) * HEAD_DIM)
        qh = pq[:, sl]
        qh = qh * cosf + pltpu.roll(qh, half, 1) * sinf
        kh = pt[sl, :]
        sw = jnp.concatenate([kh[half:, :], kh[:half, :]], axis=0)
        kh = ((kh * cost + sw * sint) * ztt_ref[hd]).astype(BF16)
        g_state = r_ref[hd]
        for c in range(tm // CHUNK):
            rows = slice(c * CHUNK, (c + 1) * CHUNK)
            qc = (qh[rows, :] * xiq_ref[hd]).astype(BF16)
            kc = kh[:, rows]
            vc = pvg[rows, sl].astype(BF16)
            s = jnp.where(causal, _dot(qc, kc), 0.0).astype(BF16)
            y = _dot(jnp.concatenate([s, qc], axis=1), jnp.concatenate([vc, g_state.astype(BF16)], axis=0))
            g_state = gch_ref[hd] * (g_state + _dot(kc, vc))
            mu = jnp.mean(y, axis=-1, keepdims=True)
            yc = y - mu
            var = jnp.mean(yc * yc, axis=-1, keepdims=True)
            yn = yc * lax.rsqrt(var + NORM_EPS) * gn_ref[hd]
            gate = pvg[rows, RET_WIDTH + hd * HEAD_DIM:RET_WIDTH + (hd + 1) * HEAD_DIM]
            yret_ref[rows, sl] = (jax.nn.silu(gate) * yn).astype(BF16)
        r_ref[hd] = g_state


def _mix_in(x2, g_pre, win, cosf, sinf, cost, sint, xiq, ztt, gch, gn, batch, seq_len, tm):
    n = x2.shape[0]
    nseq = seq_len // tm
    tok = lambda b, j: (b * nseq + j, 0)
    tile3 = lambda b, j: (b * nseq + j, 0, 0)
    const2 = lambda b, j: (0, 0)
    const3 = lambda b, j: (0, 0, 0)
    return pl.pallas_call(
        _mix_in_kernel,
        grid=(batch, nseq),
        in_specs=[
            pl.BlockSpec((tm, D_MODEL), tok),
            pl.BlockSpec((1, D_MODEL), const2),
            pl.BlockSpec((D_MODEL, 4 * RET_WIDTH + SSM_WIDTH), const2),
            pl.BlockSpec((tm, HEAD_DIM), lambda b, j: (j, 0)),
            pl.BlockSpec((tm, HEAD_DIM), lambda b, j: (j, 0)),
            pl.BlockSpec((HEAD_DIM, tm), lambda b, j: (0, j)),
            pl.BlockSpec((HEAD_DIM, tm), lambda b, j: (0, j)),
            pl.BlockSpec((RET_HEADS, CHUNK, HEAD_DIM), const3),
            pl.BlockSpec((RET_HEADS, 1, tm), const3),
            pl.BlockSpec((RET_HEADS, 1, HEAD_DIM), const3),
            pl.BlockSpec((RET_HEADS, 1, HEAD_DIM), const3),
        ],
        out_specs=[
            pl.BlockSpec((tm, RET_WIDTH), tok),
            pl.BlockSpec((1, SSM_WIDTH * (tm // CHUNK), CHUNK), tile3),
        ],
        out_shape=[
            jax.ShapeDtypeStruct((n, RET_WIDTH), BF16),
            jax.ShapeDtypeStruct((n // tm, SSM_WIDTH * (tm // CHUNK), CHUNK), F32),
        ],
        scratch_shapes=[pltpu.VMEM((RET_HEADS, HEAD_DIM, HEAD_DIM), F32),
                        pltpu.VMEM((RET_WIDTH + SSM_WIDTH, D_MODEL), BF16)],
        compiler_params=pltpu.CompilerParams(
            dimension_semantics=("arbitrary", "arbitrary"), vmem_limit_bytes=VMEM_LIMIT_BYTES),
        name="mix_in",
    )(x2, g_pre, win, cosf, sinf, cost, sint, xiq, ztt, gch, gn)


def _s5_kernel(u_ref, pwk_ref, bc_ref, tab_ref, y_ref, sm_ref, wtap_ref, wst_ref, wct_ref, ktoe_ref, *, nc, nlev):
    m = u_ref.shape[0] * u_ref.shape[2]
    npair = SSM_GROUP // 2

    def smrow(base, i):
        return sm_ref[base + i:base + i + 1, :]

    def tabrow(base, i):
        return tab_ref[0, base + i:base + i + 1, :]

    def lo_half(shape):
        return lax.broadcasted_iota(jnp.int32, shape, 1) < HALF

    def swap(x):
        return pltpu.roll(x, HALF, 1)

    br0, bi0, cr0, ci0 = (bc_ref[0, k * SSM_GROUP:(k + 1) * SSM_GROUP, :] for k in range(4))
    sm_ref[ROW_BRR:ROW_BRR + SSM_GROUP, :] = br0 + swap(br0)
    sm_ref[ROW_BII:ROW_BII + SSM_GROUP, :] = bi0 + swap(bi0)
    sm_ref[ROW_BB1:ROW_BB1 + SSM_GROUP, :] = br0 - swap(bi0)
    sm_ref[ROW_BB2:ROW_BB2 + SSM_GROUP, :] = -bi0 - swap(br0)
    sm_ref[ROW_RC1:ROW_RC1 + SSM_GROUP, :] = cr0 + swap(cr0)
    sm_ref[ROW_RC2:ROW_RC2 + SSM_GROUP, :] = ci0 + swap(ci0)

    bb1 = sm_ref[ROW_BB1:ROW_BB1 + SSM_GROUP, :]
    bb2 = sm_ref[ROW_BB2:ROW_BB2 + SSM_GROUP, :]
    mc = lambda co: bb1 * smrow(ROW_RC1, co) + bb2 * smrow(ROW_RC2, co)
    mc_even = jnp.concatenate([mc(2 * cq) for cq in range(npair)], axis=0)
    mc_odd = jnp.concatenate([mc(2 * cq + 1) for cq in range(npair)], axis=0)
    pwk = pwk_ref[0]
    lo_sq = lo_half((2 * SSM_STATE, CHUNK))
    pw_lo = jnp.where(lo_sq, pwk, 0.0)
    pw_hi = jnp.where(lo_sq, 0.0, swap(pwk))
    wtap_ref[...] = _dot_3pass(mc_even, pw_lo) + _dot_3pass(mc_odd, pw_hi)

    sgn = jnp.where(lo_half((HALF, LANES)), -1.0, 1.0)
    pwt = pwk.T[:HALF, :]
    rev = (lax.broadcasted_iota(jnp.int32, (HALF, HALF), 0)
           + lax.broadcasted_iota(jnp.int32, (HALF, HALF), 1)) == HALF - 1
    a1 = _dot_3pass(jnp.where(rev, 1.0, 0.0), pwt)
    a2 = swap(a1) * sgn
    for ci in range(SSM_GROUP):
        wst_ref[ci * HALF:(ci + 1) * HALF, :] = (
            a1 * smrow(ROW_BRR, ci) + a2 * smrow(ROW_BII, ci)).astype(BF16)
    lam_row = pwt[1:2, :]
    lo_row = lo_half((1, LANES))
    lam_r = jnp.where(lo_row, lam_row, swap(lam_row))
    lam_i = jnp.where(lo_row, -swap(lam_row), lam_row)
    p1t = pwt * lam_r + swap(pwt) * lam_i
    b1 = p1t * (-sgn)
    b2 = -swap(p1t)
    for co in range(SSM_GROUP):
        wct_ref[co * HALF:(co + 1) * HALF, :] = (
            b1 * smrow(ROW_RC1, co) + b2 * smrow(ROW_RC2, co)).astype(BF16)

    u = [u_ref[:, ci].reshape(m, CHUNK) for ci in range(SSM_GROUP)]
    lo_m = lo_half((m, LANES))
    a_even = [jnp.where(lo_m, u[2 * cp], swap(u[2 * cp + 1])).astype(BF16) for cp in range(npair)]
    a_odd = [jnp.where(lo_m, swap(u[2 * cp]), u[2 * cp + 1]).astype(BF16) for cp in range(npair)]
    a = jnp.concatenate([jnp.concatenate(a_even, axis=1), jnp.concatenate(a_odd, axis=1)], axis=0)
    st = _dot(a, wst_ref[...])
    s_even, s_odd = st[:m, :], st[m:, :]

    def cmul(x, xs, base, i):
        lr, li = tabrow(base, i), tabrow(base, i + 1)
        return x * lr + xs * li, xs * lr - x * li

    se_s = swap(s_even)
    st, sts = cmul(s_even, se_s, TAB_L64, 0)
    st, sts = st + s_odd, sts + swap(s_odd)
    block_id = lax.broadcasted_iota(jnp.int32, (m, LANES), 0) % nc
    for lev in range(nlev):
        k = 1 << lev
        keep = block_id >= k
        inc, incs = cmul(jnp.where(keep, pltpu.roll(st, k, 0), 0.0), jnp.where(keep, pltpu.roll(sts, k, 0), 0.0),
                         TAB_LAM, 2 * lev)
        st, sts = st + inc, sts + incs
    first = block_id >= 1
    xprev_even = jnp.where(first, pltpu.roll(st, 1, 0), 0.0)
    xprev_even_s = jnp.where(first, pltpu.roll(sts, 1, 0), 0.0)
    xprev_odd = cmul(xprev_even, xprev_even_s, TAB_L64, 0)[0] + s_even
    xprev = jnp.concatenate([xprev_even, xprev_odd], axis=0).astype(BF16)

    hrow = lax.broadcasted_iota(jnp.int32, (HALF, CHUNK), 0)
    hcol = lax.broadcasted_iota(jnp.int32, (HALF, CHUNK), 1) % HALF
    valid = hcol >= hrow
    nblk = MXU_DIM // CHUNK
    for j in range(npair // nblk):
        for cc in range(nblk):
            cq = j * nblk + cc
            for ci in range(SSM_GROUP):
                tile = jnp.broadcast_to(wtap_ref[cq * SSM_GROUP + ci:cq * SSM_GROUP + ci + 1, :], (HALF, CHUNK))
                tile = pltpu.roll(tile, 0, 1, stride=1, stride_axis=0)
                ktoe_ref[j, ci * HALF:(ci + 1) * HALF, cc * CHUNK:(cc + 1) * CHUNK] = (
                    jnp.where(valid, tile, 0.0).astype(BF16))
        yj = _dot(a, ktoe_ref[j]) + _dot_nt(xprev, wct_ref[j * MXU_DIM:(j + 1) * MXU_DIM, :])
        for cc in range(nblk):
            cq = j * nblk + cc
            y_even = yj[:m, cc * CHUNK:(cc + 1) * CHUNK]
            y_odd = yj[m:, cc * CHUNK:(cc + 1) * CHUNK]
            outs = (jnp.where(lo_m, y_even, swap(y_odd)), jnp.where(lo_m, swap(y_even), y_odd))
            for i in range(2):
                co = 2 * cq + i
                yc = outs[i] + tabrow(TAB_DSK, co) * u[co]
                y_ref[:, co] = jax.nn.gelu(yc).reshape(y_ref.shape[0], y_ref.shape[2], CHUNK)


def _s5(ut4, pwk, bc, tab, nc, nlev):
    nt, _, ncs, _ = ut4.shape
    grp = lambda g: (g, 0, 0)
    ugrp = lambda g: (0, g, 0, 0)
    return pl.pallas_call(
        functools.partial(_s5_kernel, nc=nc, nlev=nlev),
        grid=(SSM_GROUPS,),
        in_specs=[
            pl.BlockSpec((nt, SSM_GROUP, ncs, CHUNK), ugrp),
            pl.BlockSpec((1, 2 * SSM_STATE, CHUNK), grp),
            pl.BlockSpec((1, 4 * SSM_GROUP, LANES), grp),
            pl.BlockSpec((1, TAB_ROWS, LANES), grp),
        ],
        out_specs=pl.BlockSpec((nt, SSM_GROUP, ncs, CHUNK), ugrp),
        out_shape=jax.ShapeDtypeStruct(ut4.shape, F32),
        scratch_shapes=[
            pltpu.VMEM((SM_ROWS, LANES), F32),
            pltpu.VMEM((SSM_GROUP * SSM_GROUP // 2, CHUNK), F32),
            pltpu.VMEM((SSM_GROUP * HALF, 2 * SSM_STATE), BF16),
            pltpu.VMEM((SSM_GROUP * HALF, 2 * SSM_STATE), BF16),
            pltpu.VMEM((SSM_GROUP * HALF // MXU_DIM, SSM_GROUP * HALF, MXU_DIM), BF16),
        ],
        compiler_params=pltpu.CompilerParams(
            dimension_semantics=("parallel",), vmem_limit_bytes=VMEM_LIMIT_BYTES),
        name="s5",
    )(ut4, pwk, bc, tab)


def _mlp_kernel(x_ref, yret_ref, yst_ref, wglu_ref, wout_ref, gpost_ref, gpre2_ref, gpost2_ref,
                w1_ref, w2_ref, o_ref, *, ff_chunk, row_splits):
    tm = x_ref.shape[0]
    ncs = tm // CHUNK
    rs, cps = tm // row_splits, ncs // row_splits
    x1s, hs = [], []
    for sp in range(row_splits):
        rows = slice(sp * rs, (sp + 1) * rs)
        yst = jnp.concatenate([yst_ref[0, pl.ds(c, SSM_WIDTH, stride=ncs), :].astype(BF16)
                               for c in range(sp * cps, (sp + 1) * cps)], axis=1)
        glu = _dot_tn(yst, wglu_ref[...])
        yssm = (glu[:, :SSM_WIDTH] * jax.nn.sigmoid(glu[:, SSM_WIDTH:])).astype(BF16)
        mix = _dot(yret_ref[rows, :], wout_ref[:RET_WIDTH, :]) + _dot(yssm, wout_ref[RET_WIDTH:, :])
        x1 = x_ref[rows, :] + _rms(mix, gpost_ref[...])
        x1s.append(x1)
        hs.append(_rms(x1, gpre2_ref[...]).astype(BF16))
    mms = [None] * row_splits
    for c in range(D_FF // ff_chunk):
        for sp in range(row_splits):
            f = jnp.maximum(_dot(hs[sp], w1_ref[:, c * ff_chunk:(c + 1) * ff_chunk]), 0.0)
            part = _dot((f * f).astype(BF16), w2_ref[c * ff_chunk:(c + 1) * ff_chunk, :])
            mms[sp] = part if mms[sp] is None else mms[sp] + part
    for sp in range(row_splits):
        o_ref[sp * rs:(sp + 1) * rs, :] = x1s[sp] + _rms(mms[sp], gpost2_ref[...])


def _mlp(x2, yret, yst, wglu, wout, gpost, gpre2, gpost2, w1, w2, tm, ff_chunk):
    n = x2.shape[0]
    const = lambda i: (0, 0)
    once = pl.Buffered(1)
    return pl.pallas_call(
        functools.partial(_mlp_kernel, ff_chunk=ff_chunk, row_splits=MLP_ROW_SPLITS),
        grid=(n // tm,),
        in_specs=[
            pl.BlockSpec((tm, D_MODEL), lambda i: (i, 0)),
            pl.BlockSpec((tm, RET_WIDTH), lambda i: (i, 0)),
            pl.BlockSpec((1, SSM_WIDTH * (tm // CHUNK), CHUNK), lambda i: (i, 0, 0)),
            pl.BlockSpec((SSM_WIDTH, 2 * SSM_WIDTH), const, pipeline_mode=once),
            pl.BlockSpec((D_MODEL, D_MODEL), const, pipeline_mode=once),
            pl.BlockSpec((1, D_MODEL), const),
            pl.BlockSpec((1, D_MODEL), const),
            pl.BlockSpec((1, D_MODEL), const),
            pl.BlockSpec((D_MODEL, D_FF), const, pipeline_mode=once),
            pl.BlockSpec((D_FF, D_MODEL), const, pipeline_mode=once),
        ],
        out_specs=pl.BlockSpec((tm, D_MODEL), lambda i: (i, 0)),
        out_shape=jax.ShapeDtypeStruct((n, D_MODEL), F32),
        compiler_params=pltpu.CompilerParams(
            dimension_semantics=("parallel",), vmem_limit_bytes=VMEM_LIMIT_BYTES),
        name="mlp",
    )(x2, yret, yst, wglu, wout, gpost, gpre2, gpost2, w1, w2)


def _rope_tables(seq_len):
    half = HEAD_DIM // 2
    inv_freq = ROPE_BASE ** (-np.arange(half, dtype=np.float64) / half)
    ang = np.arange(seq_len, dtype=np.float64)[:, None] * inv_freq[None, :]
    cos, sin = np.cos(ang), np.sin(ang)
    cosf = np.concatenate([cos, cos], axis=1)
    sinf = np.concatenate([-sin, sin], axis=1)
    kscale = HEAD_DIM ** -0.5
    f32 = lambda t: jnp.asarray(np.ascontiguousarray(t), dtype=F32)
    return f32(cosf), f32(sinf), f32(cosf.T * kscale), f32(sinf.T * kscale)


def _retention_tables(tm):
    log_gamma = np.log(1.0 - np.exp(np.linspace(math.log(1.0 / 32), math.log(1.0 / 512), RET_HEADS)))
    idx = np.arange(CHUNK, dtype=np.float64)
    xiq = np.exp((idx + 1.0 - CHUNK)[None, :] * log_gamma[:, None])
    zeta = np.exp((CHUNK - 1 - idx)[None, :] * log_gamma[:, None])
    gch = np.exp(CHUNK * log_gamma)
    xiq_b = np.broadcast_to(xiq[:, :, None], (RET_HEADS, CHUNK, HEAD_DIM))
    ztt = np.tile(zeta, (1, tm // CHUNK))[:, None, :]
    gch_b = np.broadcast_to(gch[:, None, None], (RET_HEADS, 1, HEAD_DIM))
    f32 = lambda t: jnp.asarray(np.ascontiguousarray(t), dtype=F32)
    return f32(xiq_b), f32(ztt), f32(gch_b)


def _s5_tables(lam_re, lam_im, log_dt, b_re, b_im, c_re, c_im, d_skip, nlev):
    assert nlev <= MAX_SCAN_LEVELS
    a = jnp.minimum(lam_re.astype(F32), -1e-4)
    b = lam_im.astype(F32)
    dt = jnp.exp(log_dt.astype(F32))[:, None]
    tau = jnp.arange(CHUNK, dtype=F32)
    mag = jnp.exp((a * dt)[:, :, None] * tau)
    ph = (b * dt)[:, :, None] * tau
    pr, pi = mag * jnp.cos(ph), mag * jnp.sin(ph)
    lr1, li = pr[:, :, 1] - 1.0, pi[:, :, 1]
    den = a * a + b * b
    c0r, c0i = (lr1 * a + li * b) / den, (li * a - lr1 * b) / den
    bre, bim = b_re.astype(F32), b_im.astype(F32)
    br = jnp.swapaxes(c0r[:, :, None] * bre - c0i[:, :, None] * bim, 1, 2)
    bi = jnp.swapaxes(c0r[:, :, None] * bim + c0i[:, :, None] * bre, 1, 2)
    cr, ci = c_re.astype(F32), c_im.astype(F32)

    pwk = jnp.concatenate([pr, pi], axis=1)
    cat = lambda x, y: jnp.concatenate([x, y], axis=-1)

    lam_rows = []
    for step in [float(CHUNK * (1 << lev)) for lev in range(MAX_SCAN_LEVELS)] + [float(HALF)]:
        mg = jnp.exp(a * dt * step)
        re, im = mg * jnp.cos(b * dt * step), mg * jnp.sin(b * dt * step)
        lam_rows += [cat(re, re), cat(-im, im)]
    dsk = jnp.broadcast_to(d_skip.astype(F32).reshape(SSM_GROUPS, SSM_GROUP, 1), (SSM_GROUPS, SSM_GROUP, LANES))
    tab = jnp.concatenate([jnp.stack(lam_rows, axis=1), dsk], axis=1)
    bc = jnp.concatenate([br, bi, cr, ci], axis=1)
    bc = jnp.pad(bc, ((0, 0), (0, 0), (0, LANES - SSM_STATE)))
    return pwk, bc, tab


def _tiles(seq_len):
    tm_mix = min(1024, seq_len)
    tm_mlp = tm_mix
    ff_chunk = 1024
    assert seq_len % CHUNK == 0 and seq_len % tm_mix == 0
    return tm_mix, tm_mlp, ff_chunk


def kernel(x, norm_mix_pre, norm_mix_post, w_in, ret_gn_gain, ssm_lambda_re, ssm_lambda_im, ssm_log_dt,
           ssm_b_re, ssm_b_im, ssm_c_re, ssm_c_im, ssm_d, w_glu, w_out, norm_mlp_pre, norm_mlp_post,
           w_ff1, w_ff2):
    batch, seq_len, _ = x.shape
    depth = w_in.shape[0]
    n = batch * seq_len
    nc = seq_len // CHUNK
    nlev = max(1, (nc - 1).bit_length())
    tm_mix, tm_mlp, ff_chunk = _tiles(seq_len)

    cosf, sinf, cost, sint = _rope_tables(seq_len)
    xiq, ztt, gch = _retention_tables(tm_mix)
    x2 = x.reshape(n, D_MODEL)
    for i in range(depth):
        win = w_in[i].astype(BF16)
        gn = ret_gn_gain[i].astype(F32).reshape(RET_HEADS, 1, HEAD_DIM)
        yret, ut = _mix_in(x2, norm_mix_pre[i][None, :], win, cosf, sinf, cost, sint,
                           xiq, ztt, gch, gn, batch, seq_len, tm_mix)
        pwk, bc, tab = _s5_tables(
            ssm_lambda_re[i], ssm_lambda_im[i], ssm_log_dt[i], ssm_b_re[i], ssm_b_im[i],
            ssm_c_re[i], ssm_c_im[i], ssm_d[i], nlev)
        ncs = tm_mix // CHUNK
        yst = _s5(ut.reshape(n // tm_mix, SSM_WIDTH, ncs, CHUNK), pwk, bc, tab, nc, nlev)
        yst = yst.reshape(n // tm_mlp, SSM_WIDTH * ncs, CHUNK)
        x2 = _mlp(x2, yret, yst, w_glu[i].astype(BF16), w_out[i].astype(BF16),
                  norm_mix_post[i][None, :], norm_mlp_pre[i][None, :], norm_mlp_post[i][None, :],
                  w_ff1[i].astype(BF16), w_ff2[i].astype(BF16), tm_mlp, ff_chunk)
    return x2.reshape(batch, seq_len, D_MODEL)
```

```python
import functools
import math

import jax
import jax.numpy as jnp
import numpy as np
from jax import lax
from jax.experimental import pallas as pl
from jax.experimental.pallas import tpu as pltpu

D_MODEL = 1024
RET_WIDTH = 512
RET_HEADS = 4
HEAD_DIM = 128
CHUNK = 128
HALF = CHUNK // 2
ROPE_BASE = 10000.0
SSM_WIDTH = 512
SSM_GROUP = 16
SSM_GROUPS = 32
SSM_STATE = 64
D_FF = 4096
NORM_EPS = 1e-6
LANES = 128
MXU_DIM = 256
MLP_ROW_SPLITS = 2

BF16 = jnp.bfloat16
F32 = jnp.float32

VMEM_LIMIT_BYTES = 56 * 1024 * 1024

ROW_BRR, ROW_BII, ROW_BB1, ROW_BB2, ROW_RC1, ROW_RC2, SM_ROWS = 0, 16, 32, 48, 64, 80, 96
TAB_LAM, TAB_L64, TAB_DSK, TAB_ROWS = 0, 14, 16, 32
MAX_SCAN_LEVELS = (TAB_L64 - TAB_LAM) // 2


def _dot(a, b):
    return jnp.dot(a, b, preferred_element_type=F32)


def _dot_nt(a, b):
    return lax.dot_general(a, b, (((1,), (1,)), ((), ())), preferred_element_type=F32)


def _dot_tn(a, b):
    return lax.dot_general(a, b, (((0,), (0,)), ((), ())), preferred_element_type=F32)


def _dot_3pass(a, b):
    a_hi = a.astype(BF16)
    a_lo = (a - a_hi.astype(F32)).astype(BF16)
    b_hi = b.astype(BF16)
    b_lo = (b - b_hi.astype(F32)).astype(BF16)
    return _dot(a_hi, b_hi) + (_dot(a_hi, b_lo) + _dot(a_lo, b_hi))


def _rms(x, g):
    ms = jnp.mean(x * x, axis=-1, keepdims=True)
    return x * lax.rsqrt(ms + NORM_EPS) * g


def _mix_in_kernel(x_ref, g_ref, winf_ref, cosf_ref, sinf_ref, cost_ref, sint_ref,
                   xiq_ref, ztt_ref, gch_ref, gn_ref, wglu_ref, wout_ref, w1_ref, w2_ref,
                   yret_ref, ut_ref, wglub_ref, woutb_ref, w1b_ref, w2b_ref, r_ref, win_ref, wkut_ref):
    @pl.when((pl.program_id(0) == 0) & (pl.program_id(1) == 0))
    def _():
        for c0 in range(0, win_ref.shape[1], RET_WIDTH):
            win_ref[:, c0:c0 + RET_WIDTH] = winf_ref[:, c0:c0 + RET_WIDTH].astype(BF16)
        for blk in range((RET_WIDTH + SSM_WIDTH) // LANES):
            c0 = RET_WIDTH + blk * LANES if blk * LANES < RET_WIDTH else 3 * RET_WIDTH + blk * LANES
            wkut_ref[blk * LANES:(blk + 1) * LANES, :] = winf_ref[:, c0:c0 + LANES].T.astype(BF16)

    wglub_ref[...] = wglu_ref[...].astype(BF16)
    woutb_ref[...] = wout_ref[...].astype(BF16)
    w1b_ref[...] = w1_ref[...].astype(BF16)
    w2b_ref[...] = w2_ref[...].astype(BF16)

    @pl.when(pl.program_id(1) == 0)
    def _():
        r_ref[...] = jnp.zeros_like(r_ref)

    tm = x_ref.shape[0]
    half = HEAD_DIM // 2
    h = _rms(x_ref[...], g_ref[...]).astype(BF16)
    pq = _dot(h, win_ref[:, :RET_WIDTH])
    pvg = _dot(h, win_ref[:, 2 * RET_WIDTH:4 * RET_WIDTH])
    pt = _dot_nt(wkut_ref[...], h)
    ncs = tm // CHUNK
    for c in range(ncs):
        ut_ref[0, pl.ds(c, SSM_WIDTH, stride=ncs), :] = pt[RET_WIDTH:, c * CHUNK:(c + 1) * CHUNK]

    row = lax.broadcasted_iota(jnp.int32, (CHUNK, CHUNK), 0)
    col = lax.broadcasted_iota(jnp.int32, (CHUNK, CHUNK), 1)
    causal = row >= col
    cosf, sinf = cosf_ref[...], sinf_ref[...]
    cost, sint = cost_ref[...], sint_ref[...]
    for hd in range(RET_HEADS):
        sl = slice(hd * HEAD_DIM, (hd + 1) * HEAD_DIM)
        qh = pq[:, sl]
        qh = qh * cosf + pltpu.roll(qh, half, 1) * sinf
        kh = pt[sl, :]
        sw = jnp.concatenate([kh[half:, :], kh[:half, :]], axis=0)
        kh = ((kh * cost + sw * sint) * ztt_ref[hd]).astype(BF16)
        g_state = r_ref[hd]
        for c in range(tm // CHUNK):
            rows = slice(c * CHUNK, (c + 1) * CHUNK)
            qc = (qh[rows, :] * xiq_ref[hd]).astype(BF16)
            kc = kh[:, rows]
            vc = pvg[rows, sl].astype(BF16)
            s = jnp.where(causal, _dot(qc, kc), 0.0).astype(BF16)
            y = _dot(jnp.concatenate([s, qc], axis=1), jnp.concatenate([vc, g_state.astype(BF16)], axis=0))
            g_state = gch_ref[hd] * (g_state + _dot(kc, vc))
            mu = jnp.mean(y, axis=-1, keepdims=True)
            yc = y - mu
            var = jnp.mean(yc * yc, axis=-1, keepdims=True)
            yn = yc * lax.rsqrt(var + NORM_EPS) * gn_ref[hd]
            gate = pvg[rows, RET_WIDTH + hd * HEAD_DIM:RET_WIDTH + (hd + 1) * HEAD_DIM]
            yret_ref[rows, sl] = (jax.nn.silu(gate) * yn).astype(BF16)
        r_ref[hd] = g_state


def _mix_in(x2, g_pre, winf, cosf, sinf, cost, sint, xiq, ztt, gch, gn, wglu, wout, w1, w2, batch, seq_len, tm):
    n = x2.shape[0]
    nseq = seq_len // tm
    nsteps = batch * nseq
    tok = lambda b, j: (b * nseq + j, 0)
    slab = lambda w: pl.BlockSpec((w.shape[0] // nsteps, w.shape[1]), tok)
    later = (wglu, wout, w1, w2)
    assert all(w.shape[0] % (16 * nsteps) == 0 for w in later)
    tile3 = lambda b, j: (b * nseq + j, 0, 0)
    const2 = lambda b, j: (0, 0)
    const3 = lambda b, j: (0, 0, 0)
    return pl.pallas_call(
        _mix_in_kernel,
        grid=(batch, nseq),
        in_specs=[
            pl.BlockSpec((tm, D_MODEL), tok),
            pl.BlockSpec((1, D_MODEL), const2),
            pl.BlockSpec((D_MODEL, 4 * RET_WIDTH + SSM_WIDTH), const2, pipeline_mode=pl.Buffered(1)),
            pl.BlockSpec((tm, HEAD_DIM), lambda b, j: (j, 0)),
            pl.BlockSpec((tm, HEAD_DIM), lambda b, j: (j, 0)),
            pl.BlockSpec((HEAD_DIM, tm), lambda b, j: (0, j)),
            pl.BlockSpec((HEAD_DIM, tm), lambda b, j: (0, j)),
            pl.BlockSpec((RET_HEADS, CHUNK, HEAD_DIM), const3),
            pl.BlockSpec((RET_HEADS, 1, tm), const3),
            pl.BlockSpec((RET_HEADS, 1, HEAD_DIM), const3),
            pl.BlockSpec((RET_HEADS, 1, HEAD_DIM), const3),
        ] + [slab(w) for w in later],
        out_specs=[
            pl.BlockSpec((tm, RET_WIDTH), tok),
            pl.BlockSpec((1, SSM_WIDTH * (tm // CHUNK), CHUNK), tile3),
        ] + [slab(w) for w in later],
        out_shape=[
            jax.ShapeDtypeStruct((n, RET_WIDTH), BF16),
            jax.ShapeDtypeStruct((n // tm, SSM_WIDTH * (tm // CHUNK), CHUNK), F32),
        ] + [jax.ShapeDtypeStruct(w.shape, BF16) for w in later],
        scratch_shapes=[pltpu.VMEM((RET_HEADS, HEAD_DIM, HEAD_DIM), F32),
                        pltpu.VMEM((D_MODEL, 4 * RET_WIDTH + SSM_WIDTH), BF16),
                        pltpu.VMEM((RET_WIDTH + SSM_WIDTH, D_MODEL), BF16)],
        compiler_params=pltpu.CompilerParams(
            dimension_semantics=("arbitrary", "arbitrary"), vmem_limit_bytes=VMEM_LIMIT_BYTES),
        name="mix_in",
    )(x2, g_pre, winf, cosf, sinf, cost, sint, xiq, ztt, gch, gn, *later)


def _s5_kernel(u_ref, pwk_ref, bc_ref, tab_ref, y_ref, sm_ref, wtap_ref, wst_ref, wct_ref, ktoe_ref, *, nc, nlev):
    m = u_ref.shape[0] * u_ref.shape[2]
    npair = SSM_GROUP // 2

    def smrow(base, i):
        return sm_ref[base + i:base + i + 1, :]

    def tabrow(base, i):
        return tab_ref[0, base + i:base + i + 1, :]

    def lo_half(shape):
        return lax.broadcasted_iota(jnp.int32, shape, 1) < HALF

    def swap(x):
        return pltpu.roll(x, HALF, 1)

    br0, bi0, cr0, ci0 = (bc_ref[0, k * SSM_GROUP:(k + 1) * SSM_GROUP, :] for k in range(4))
    sm_ref[ROW_BRR:ROW_BRR + SSM_GROUP, :] = br0 + swap(br0)
    sm_ref[ROW_BII:ROW_BII + SSM_GROUP, :] = bi0 + swap(bi0)
    sm_ref[ROW_BB1:ROW_BB1 + SSM_GROUP, :] = br0 - swap(bi0)
    sm_ref[ROW_BB2:ROW_BB2 + SSM_GROUP, :] = -bi0 - swap(br0)
    sm_ref[ROW_RC1:ROW_RC1 + SSM_GROUP, :] = cr0 + swap(cr0)
    sm_ref[ROW_RC2:ROW_RC2 + SSM_GROUP, :] = ci0 + swap(ci0)

    bb1 = sm_ref[ROW_BB1:ROW_BB1 + SSM_GROUP, :]
    bb2 = sm_ref[ROW_BB2:ROW_BB2 + SSM_GROUP, :]
    mc = lambda co: bb1 * smrow(ROW_RC1, co) + bb2 * smrow(ROW_RC2, co)
    mc_even = jnp.concatenate([mc(2 * cq) for cq in range(npair)], axis=0)
    mc_odd = jnp.concatenate([mc(2 * cq + 1) for cq in range(npair)], axis=0)
    pwk = pwk_ref[0]
    lo_sq = lo_half((2 * SSM_STATE, CHUNK))
    pw_lo = jnp.where(lo_sq, pwk, 0.0)
    pw_hi = jnp.where(lo_sq, 0.0, swap(pwk))
    wtap_ref[...] = _dot_3pass(mc_even, pw_lo) + _dot_3pass(mc_odd, pw_hi)

    sgn = jnp.where(lo_half((HALF, LANES)), -1.0, 1.0)
    pwt = pwk.T[:HALF, :]
    rev = (lax.broadcasted_iota(jnp.int32, (HALF, HALF), 0)
           + lax.broadcasted_iota(jnp.int32, (HALF, HALF), 1)) == HALF - 1
    a1 = _dot_3pass(jnp.where(rev, 1.0, 0.0), pwt)
    a2 = swap(a1) * sgn
    for ci in range(SSM_GROUP):
        wst_ref[ci * HALF:(ci + 1) * HALF, :] = (
            a1 * smrow(ROW_BRR, ci) + a2 * smrow(ROW_BII, ci)).astype(BF16)
    lam_row = pwt[1:2, :]
    lo_row = lo_half((1, LANES))
    lam_r = jnp.where(lo_row, lam_row, swap(lam_row))
    lam_i = jnp.where(lo_row, -swap(lam_row), lam_row)
    p1t = pwt * lam_r + swap(pwt) * lam_i
    b1 = p1t * (-sgn)
    b2 = -swap(p1t)
    for co in range(SSM_GROUP):
        wct_ref[co * HALF:(co + 1) * HALF, :] = (
            b1 * smrow(ROW_RC1, co) + b2 * smrow(ROW_RC2, co)).astype(BF16)

    u = [u_ref[:, ci].reshape(m, CHUNK) for ci in range(SSM_GROUP)]
    lo_m = lo_half((m, LANES))
    a_even = [jnp.where(lo_m, u[2 * cp], swap(u[2 * cp + 1])).astype(BF16) for cp in range(npair)]
    a_odd = [jnp.where(lo_m, swap(u[2 * cp]), u[2 * cp + 1]).astype(BF16) for cp in range(npair)]
    a = jnp.concatenate([jnp.concatenate(a_even, axis=1), jnp.concatenate(a_odd, axis=1)], axis=0)
    st = _dot(a, wst_ref[...])
    s_even, s_odd = st[:m, :], st[m:, :]

    def cmul(x, xs, base, i):
        lr, li = tabrow(base, i), tabrow(base, i + 1)
        return x * lr + xs * li, xs * lr - x * li

    se_s = swap(s_even)
    st, sts = cmul(s_even, se_s, TAB_L64, 0)
    st, sts = st + s_odd, sts + swap(s_odd)
    block_id = lax.broadcasted_iota(jnp.int32, (m, LANES), 0) % nc
    for lev in range(nlev):
        k = 1 << lev
        keep = block_id >= k
        inc, incs = cmul(jnp.where(keep, pltpu.roll(st, k, 0), 0.0), jnp.where(keep, pltpu.roll(sts, k, 0), 0.0),
                         TAB_LAM, 2 * lev)
        st, sts = st + inc, sts + incs
    first = block_id >= 1
    xprev_even = jnp.where(first, pltpu.roll(st, 1, 0), 0.0)
    xprev_even_s = jnp.where(first, pltpu.roll(sts, 1, 0), 0.0)
    xprev_odd = cmul(xprev_even, xprev_even_s, TAB_L64, 0)[0] + s_even
    xprev = jnp.concatenate([xprev_even, xprev_odd], axis=0).astype(BF16)

    hrow = lax.broadcasted_iota(jnp.int32, (HALF, CHUNK), 0)
    hcol = lax.broadcasted_iota(jnp.int32, (HALF, CHUNK), 1) % HALF
    valid = hcol >= hrow
    nblk = MXU_DIM // CHUNK
    for j in range(npair // nblk):
        for cc in range(nblk):
            cq = j * nblk + cc
            for ci in range(SSM_GROUP):
                tile = jnp.broadcast_to(wtap_ref[cq * SSM_GROUP + ci:cq * SSM_GROUP + ci + 1, :], (HALF, CHUNK))
                tile = pltpu.roll(tile, 0, 1, stride=1, stride_axis=0)
                ktoe_ref[j, ci * HALF:(ci + 1) * HALF, cc * CHUNK:(cc + 1) * CHUNK] = (
                    jnp.where(valid, tile, 0.0).astype(BF16))
        yj = _dot(a, ktoe_ref[j]) + _dot_nt(xprev, wct_ref[j * MXU_DIM:(j + 1) * MXU_DIM, :])
        for cc in range(nblk):
            cq = j * nblk + cc
            y_even = yj[:m, cc * CHUNK:(cc + 1) * CHUNK]
            y_odd = yj[m:, cc * CHUNK:(cc + 1) * CHUNK]
            outs = (jnp.where(lo_m, y_even, swap(y_odd)), jnp.where(lo_m, swap(y_even), y_odd))
            for i in range(2):
                co = 2 * cq + i
                yc = outs[i] + tabrow(TAB_DSK, co) * u[co]
                y_ref[:, co] = jax.nn.gelu(yc).reshape(y_ref.shape[0], y_ref.shape[2], CHUNK)


def _s5(ut4, pwk, bc, tab, nc, nlev):
    nt, _, ncs, _ = ut4.shape
    grp = lambda g: (g, 0, 0)
    ugrp = lambda g: (0, g, 0, 0)
    return pl.pallas_call(
        functools.partial(_s5_kernel, nc=nc, nlev=nlev),
        grid=(SSM_GROUPS,),
        in_specs=[
            pl.BlockSpec((nt, SSM_GROUP, ncs, CHUNK), ugrp),
            pl.BlockSpec((1, 2 * SSM_STATE, CHUNK), grp),
            pl.BlockSpec((1, 4 * SSM_GROUP, LANES), grp),
            pl.BlockSpec((1, TAB_ROWS, LANES), grp),
        ],
        out_specs=pl.BlockSpec((nt, SSM_GROUP, ncs, CHUNK), ugrp),
        out_shape=jax.ShapeDtypeStruct(ut4.shape, F32),
        scratch_shapes=[
            pltpu.VMEM((SM_ROWS, LANES), F32),
            pltpu.VMEM((SSM_GROUP * SSM_GROUP // 2, CHUNK), F32),
            pltpu.VMEM((SSM_GROUP * HALF, 2 * SSM_STATE), BF16),
            pltpu.VMEM((SSM_GROUP * HALF, 2 * SSM_STATE), BF16),
            pltpu.VMEM((SSM_GROUP * HALF // MXU_DIM, SSM_GROUP * HALF, MXU_DIM), BF16),
        ],
        compiler_params=pltpu.CompilerParams(
            dimension_semantics=("parallel",), vmem_limit_bytes=VMEM_LIMIT_BYTES),
        name="s5",
    )(ut4, pwk, bc, tab)


def _mlp_kernel(x_ref, yret_ref, yst_ref, wglu_ref, wout_ref, gpost_ref, gpre2_ref, gpost2_ref,
                w1_ref, w2_ref, o_ref, *, ff_chunk, row_splits):
    tm = x_ref.shape[0]
    ncs = tm // CHUNK
    rs, cps = tm // row_splits, ncs // row_splits
    x1s, hs = [], []
    for sp in range(row_splits):
        rows = slice(sp * rs, (sp + 1) * rs)
        yst = jnp.concatenate([yst_ref[0, pl.ds(c, SSM_WIDTH, stride=ncs), :].astype(BF16)
                               for c in range(sp * cps, (sp + 1) * cps)], axis=1)
        glu = _dot_tn(yst, wglu_ref[...])
        yssm = (glu[:, :SSM_WIDTH] * jax.nn.sigmoid(glu[:, SSM_WIDTH:])).astype(BF16)
        mix = _dot(yret_ref[rows, :], wout_ref[:RET_WIDTH, :]) + _dot(yssm, wout_ref[RET_WIDTH:, :])
        x1 = x_ref[rows, :] + _rms(mix, gpost_ref[...])
        x1s.append(x1)
        hs.append(_rms(x1, gpre2_ref[...]).astype(BF16))
    mms = [None] * row_splits
    for c in range(D_FF // ff_chunk):
        for sp in range(row_splits):
            f = jnp.maximum(_dot(hs[sp], w1_ref[:, c * ff_chunk:(c + 1) * ff_chunk]), 0.0)
            part = _dot((f * f).astype(BF16), w2_ref[c * ff_chunk:(c + 1) * ff_chunk, :])
            mms[sp] = part if mms[sp] is None else mms[sp] + part
    for sp in range(row_splits):
        o_ref[sp * rs:(sp + 1) * rs, :] = x1s[sp] + _rms(mms[sp], gpost2_ref[...])


def _mlp(x2, yret, yst, wglu, wout, gpost, gpre2, gpost2, w1, w2, tm, ff_chunk):
    n = x2.shape[0]
    const = lambda i: (0, 0)
    once = pl.Buffered(1)
    return pl.pallas_call(
        functools.partial(_mlp_kernel, ff_chunk=ff_chunk, row_splits=MLP_ROW_SPLITS),
        grid=(n // tm,),
        in_specs=[
            pl.BlockSpec((tm, D_MODEL), lambda i: (i, 0)),
            pl.BlockSpec((tm, RET_WIDTH), lambda i: (i, 0)),
            pl.BlockSpec((1, SSM_WIDTH * (tm // CHUNK), CHUNK), lambda i: (i, 0, 0)),
            pl.BlockSpec((SSM_WIDTH, 2 * SSM_WIDTH), const, pipeline_mode=once),
            pl.BlockSpec((D_MODEL, D_MODEL), const, pipeline_mode=once),
            pl.BlockSpec((1, D_MODEL), const),
            pl.BlockSpec((1, D_MODEL), const),
            pl.BlockSpec((1, D_MODEL), const),
            pl.BlockSpec((D_MODEL, D_FF), const, pipeline_mode=once),
            pl.BlockSpec((D_FF, D_MODEL), const, pipeline_mode=once),
        ],
        out_specs=pl.BlockSpec((tm, D_MODEL), lambda i: (i, 0)),
        out_shape=jax.ShapeDtypeStruct((n, D_MODEL), F32),
        compiler_params=pltpu.CompilerParams(
            dimension_semantics=("parallel",), vmem_limit_bytes=VMEM_LIMIT_BYTES),
        name="mlp",
    )(x2, yret, yst, wglu, wout, gpost, gpre2, gpost2, w1, w2)


def _rope_tables(seq_len):
    half = HEAD_DIM // 2
    inv_freq = ROPE_BASE ** (-np.arange(half, dtype=np.float64) / half)
    ang = np.arange(seq_len, dtype=np.float64)[:, None] * inv_freq[None, :]
    cos, sin = np.cos(ang), np.sin(ang)
    cosf = np.concatenate([cos, cos], axis=1)
    sinf = np.concatenate([-sin, sin], axis=1)
    kscale = HEAD_DIM ** -0.5
    f32 = lambda t: jnp.asarray(np.ascontiguousarray(t), dtype=F32)
    return f32(cosf), f32(sinf), f32(cosf.T * kscale), f32(sinf.T * kscale)


def _retention_tables(tm):
    log_gamma = np.log(1.0 - np.exp(np.linspace(math.log(1.0 / 32), math.log(1.0 / 512), RET_HEADS)))
    idx = np.arange(CHUNK, dtype=np.float64)
    xiq = np.exp((idx + 1.0 - CHUNK)[None, :] * log_gamma[:, None])
    zeta = np.exp((CHUNK - 1 - idx)[None, :] * log_gamma[:, None])
    gch = np.exp(CHUNK * log_gamma)
    xiq_b = np.broadcast_to(xiq[:, :, None], (RET_HEADS, CHUNK, HEAD_DIM))
    ztt = np.tile(zeta, (1, tm // CHUNK))[:, None, :]
    gch_b = np.broadcast_to(gch[:, None, None], (RET_HEADS, 1, HEAD_DIM))
    f32 = lambda t: jnp.asarray(np.ascontiguousarray(t), dtype=F32)
    return f32(xiq_b), f32(ztt), f32(gch_b)


def _s5_tables(lam_re, lam_im, log_dt, b_re, b_im, c_re, c_im, d_skip, nlev):
    assert nlev <= MAX_SCAN_LEVELS
    a = jnp.minimum(lam_re.astype(F32), -1e-4)
    b = lam_im.astype(F32)
    dt = jnp.exp(log_dt.astype(F32))[:, None]
    tau = jnp.arange(CHUNK, dtype=F32)
    mag = jnp.exp((a * dt)[:, :, None] * tau)
    ph = (b * dt)[:, :, None] * tau
    pr, pi = mag * jnp.cos(ph), mag * jnp.sin(ph)
    lr1, li = pr[:, :, 1] - 1.0, pi[:, :, 1]
    den = a * a + b * b
    c0r, c0i = (lr1 * a + li * b) / den, (li * a - lr1 * b) / den
    bre, bim = b_re.astype(F32), b_im.astype(F32)
    br = jnp.swapaxes(c0r[:, :, None] * bre - c0i[:, :, None] * bim, 1, 2)
    bi = jnp.swapaxes(c0r[:, :, None] * bim + c0i[:, :, None] * bre, 1, 2)
    cr, ci = c_re.astype(F32), c_im.astype(F32)

    pwk = jnp.concatenate([pr, pi], axis=1)
    cat = lambda x, y: jnp.concatenate([x, y], axis=-1)

    lam_rows = []
    for step in [float(CHUNK * (1 << lev)) for lev in range(MAX_SCAN_LEVELS)] + [float(HALF)]:
        mg = jnp.exp(a * dt * step)
        re, im = mg * jnp.cos(b * dt * step), mg * jnp.sin(b * dt * step)
        lam_rows += [cat(re, re), cat(-im, im)]
    dsk = jnp.broadcast_to(d_skip.astype(F32).reshape(SSM_GROUPS, SSM_GROUP, 1), (SSM_GROUPS, SSM_GROUP, LANES))
    tab = jnp.concatenate([jnp.stack(lam_rows, axis=1), dsk], axis=1)
    bc = jnp.concatenate([br, bi, cr, ci], axis=1)
    bc = jnp.pad(bc, ((0, 0), (0, 0), (0, LANES - SSM_STATE)))
    return pwk, bc, tab


def _tiles(seq_len):
    tm_mix = min(1024, seq_len)
    tm_mlp = tm_mix
    ff_chunk = 1024
    assert seq_len % CHUNK == 0 and seq_len % tm_mix == 0
    return tm_mix, tm_mlp, ff_chunk


def kernel(x, norm_mix_pre, norm_mix_post, w_in, ret_gn_gain, ssm_lambda_re, ssm_lambda_im, ssm_log_dt,
           ssm_b_re, ssm_b_im, ssm_c_re, ssm_c_im, ssm_d, w_glu, w_out, norm_mlp_pre, norm_mlp_post,
           w_ff1, w_ff2):
    batch, seq_len, _ = x.shape
    depth = w_in.shape[0]
    n = batch * seq_len
    nc = seq_len // CHUNK
    nlev = max(1, (nc - 1).bit_length())
    tm_mix, tm_mlp, ff_chunk = _tiles(seq_len)

    cosf, sinf, cost, sint = _rope_tables(seq_len)
    xiq, ztt, gch = _retention_tables(tm_mix)
    x2 = x.reshape(n, D_MODEL)
    for i in range(depth):
        gn = ret_gn_gain[i].astype(F32).reshape(RET_HEADS, 1, HEAD_DIM)
        yret, ut, wglu, wout, w1, w2 = _mix_in(
            x2, norm_mix_pre[i][None, :], w_in[i], cosf, sinf, cost, sint, xiq, ztt, gch, gn,
            w_glu[i], w_out[i], w_ff1[i], w_ff2[i], batch, seq_len, tm_mix)
        pwk, bc, tab = _s5_tables(
            ssm_lambda_re[i], ssm_lambda_im[i], ssm_log_dt[i], ssm_b_re[i], ssm_b_im[i],
            ssm_c_re[i], ssm_c_im[i], ssm_d[i], nlev)
        ncs = tm_mix // CHUNK
        yst = _s5(ut.reshape(n // tm_mix, SSM_WIDTH, ncs, CHUNK), pwk, bc, tab, nc, nlev)
        yst = yst.reshape(n // tm_mlp, SSM_WIDTH * ncs, CHUNK)
        x2 = _mlp(x2, yret, yst, wglu, wout,
                  norm_mix_post[i][None, :], norm_mlp_pre[i][None, :], norm_mlp_post[i][None, :],
                  w1, w2, tm_mlp, ff_chunk)
    return x2.reshape(batch, seq_len, D_MODEL)
```

```python
import functools
import math

import jax
import jax.numpy as jnp
import numpy as np
from jax import lax
from jax.experimental import pallas as pl
from jax.experimental.pallas import tpu as pltpu

D_MODEL = 1024
RET_WIDTH = 512
RET_HEADS = 4
HEAD_DIM = 128
CHUNK = 128
HALF = CHUNK // 2
ROPE_BASE = 10000.0
SSM_WIDTH = 512
SSM_GROUP = 16
SSM_GROUPS = 32
SSM_STATE = 64
D_FF = 4096
NORM_EPS = 1e-6
LANES = 128
MXU_DIM = 256
MLP_ROW_SPLITS = 2

BF16 = jnp.bfloat16
F32 = jnp.float32

VMEM_LIMIT_BYTES = 56 * 1024 * 1024

ROW_BRR, ROW_BII, ROW_BB1, ROW_BB2, ROW_RC1, ROW_RC2, SM_ROWS = 0, 16, 32, 48, 64, 80, 96
TAB_LAM, TAB_L64, TAB_DSK, TAB_ROWS = 0, 14, 16, 24
MAX_SCAN_LEVELS = (TAB_L64 - TAB_LAM) // 2


def _dot(a, b):
    return jnp.dot(a, b, preferred_element_type=F32)


def _dot_nt(a, b):
    return lax.dot_general(a, b, (((1,), (1,)), ((), ())), preferred_element_type=F32)


def _dot_tn(a, b):
    return lax.dot_general(a, b, (((0,), (0,)), ((), ())), preferred_element_type=F32)


def _dot_3pass(a, b):
    a_hi = a.astype(BF16)
    a_lo = (a - a_hi.astype(F32)).astype(BF16)
    b_hi = b.astype(BF16)
    b_lo = (b - b_hi.astype(F32)).astype(BF16)
    return _dot(a_hi, b_hi) + (_dot(a_hi, b_lo) + _dot(a_lo, b_hi))


def _rms(x, g):
    ms = jnp.mean(x * x, axis=-1, keepdims=True)
    return x * lax.rsqrt(ms + NORM_EPS) * g


def _mix_in_kernel(x_ref, g_ref, winf_ref, cosf_ref, sinf_ref, cost_ref, sint_ref,
                   xiq_ref, ztt_ref, gch_ref, gn_ref, wglu_ref, wout_ref, w1_ref, w2_ref,
                   yret_ref, ut_ref, wglub_ref, woutb_ref, w1b_ref, w2b_ref, r_ref, win_ref, wkut_ref):
    @pl.when((pl.program_id(0) == 0) & (pl.program_id(1) == 0))
    def _():
        for c0 in range(0, win_ref.shape[1], RET_WIDTH):
            win_ref[:, c0:c0 + RET_WIDTH] = winf_ref[:, c0:c0 + RET_WIDTH].astype(BF16)
        for blk in range((RET_WIDTH + SSM_WIDTH) // LANES):
            c0 = RET_WIDTH + blk * LANES if blk * LANES < RET_WIDTH else 3 * RET_WIDTH + blk * LANES
            wkut_ref[blk * LANES:(blk + 1) * LANES, :] = winf_ref[:, c0:c0 + LANES].T.astype(BF16)

    wglub_ref[...] = wglu_ref[...].astype(BF16)
    woutb_ref[...] = wout_ref[...].astype(BF16)
    w1b_ref[...] = w1_ref[...].astype(BF16)
    w2b_ref[...] = w2_ref[...].astype(BF16)

    @pl.when(pl.program_id(1) == 0)
    def _():
        r_ref[...] = jnp.zeros_like(r_ref)

    tm = x_ref.shape[0]
    half = HEAD_DIM // 2
    h = _rms(x_ref[...], g_ref[...]).astype(BF16)
    pq = _dot(h, win_ref[:, :RET_WIDTH])
    pvg = _dot(h, win_ref[:, 2 * RET_WIDTH:4 * RET_WIDTH])
    pt = _dot_nt(wkut_ref[...], h)
    ncs = tm // CHUNK
    for c in range(ncs):
        ut_ref[0, pl.ds(c, SSM_WIDTH, stride=ncs), :] = pt[RET_WIDTH:, c * CHUNK:(c + 1) * CHUNK]

    row = lax.broadcasted_iota(jnp.int32, (CHUNK, CHUNK), 0)
    col = lax.broadcasted_iota(jnp.int32, (CHUNK, CHUNK), 1)
    causal = row >= col
    cosf, sinf = cosf_ref[...], sinf_ref[...]
    cost, sint = cost_ref[...], sint_ref[...]
    for hd in range(RET_HEADS):
        sl = slice(hd * HEAD_DIM, (hd + 1) * HEAD_DIM)
        qh = pq[:, sl]
        qh = qh * cosf + pltpu.roll(qh, half, 1) * sinf
        kh = pt[sl, :]
        sw = jnp.concatenate([kh[half:, :], kh[:half, :]], axis=0)
        kh = ((kh * cost + sw * sint) * ztt_ref[hd]).astype(BF16)
        g_state = r_ref[hd]
        for c in range(tm // CHUNK):
            rows = slice(c * CHUNK, (c + 1) * CHUNK)
            qc = (qh[rows, :] * xiq_ref[hd]).astype(BF16)
            kc = kh[:, rows]
            vc = pvg[rows, sl].astype(BF16)
            s = jnp.where(causal, _dot(qc, kc), 0.0).astype(BF16)
            y = _dot(jnp.concatenate([s, qc], axis=1), jnp.concatenate([vc, g_state.astype(BF16)], axis=0))
            g_state = gch_ref[hd] * (g_state + _dot(kc, vc))
            mu = jnp.mean(y, axis=-1, keepdims=True)
            yc = y - mu
            var = jnp.mean(yc * yc, axis=-1, keepdims=True)
            yn = yc * lax.rsqrt(var + NORM_EPS) * gn_ref[hd]
            gate = pvg[rows, RET_WIDTH + hd * HEAD_DIM:RET_WIDTH + (hd + 1) * HEAD_DIM]
            yret_ref[rows, sl] = (jax.nn.silu(gate) * yn).astype(BF16)
        r_ref[hd] = g_state


def _mix_in(x2, g_pre, winf, cosf, sinf, cost, sint, xiq, ztt, gch, gn, wglu, wout, w1, w2, batch, seq_len, tm):
    n = x2.shape[0]
    nseq = seq_len // tm
    nsteps = batch * nseq
    tok = lambda b, j: (b * nseq + j, 0)
    slab = lambda w: pl.BlockSpec((w.shape[0] // nsteps, w.shape[1]), tok)
    later = (wglu, wout, w1, w2)
    assert all(w.shape[0] % (16 * nsteps) == 0 for w in later)
    tile3 = lambda b, j: (b * nseq + j, 0, 0)
    const2 = lambda b, j: (0, 0)
    const3 = lambda b, j: (0, 0, 0)
    return pl.pallas_call(
        _mix_in_kernel,
        grid=(batch, nseq),
        in_specs=[
            pl.BlockSpec((tm, D_MODEL), tok),
            pl.BlockSpec((1, D_MODEL), const2),
            pl.BlockSpec((D_MODEL, 4 * RET_WIDTH + SSM_WIDTH), const2, pipeline_mode=pl.Buffered(1)),
            pl.BlockSpec((tm, HEAD_DIM), lambda b, j: (j, 0)),
            pl.BlockSpec((tm, HEAD_DIM), lambda b, j: (j, 0)),
            pl.BlockSpec((HEAD_DIM, tm), lambda b, j: (0, j)),
            pl.BlockSpec((HEAD_DIM, tm), lambda b, j: (0, j)),
            pl.BlockSpec((RET_HEADS, CHUNK, HEAD_DIM), const3),
            pl.BlockSpec((RET_HEADS, 1, tm), const3),
            pl.BlockSpec((RET_HEADS, 1, HEAD_DIM), const3),
            pl.BlockSpec((RET_HEADS, 1, HEAD_DIM), const3),
        ] + [slab(w) for w in later],
        out_specs=[
            pl.BlockSpec((tm, RET_WIDTH), tok),
            pl.BlockSpec((1, SSM_WIDTH * (tm // CHUNK), CHUNK), tile3),
        ] + [slab(w) for w in later],
        out_shape=[
            jax.ShapeDtypeStruct((n, RET_WIDTH), BF16),
            jax.ShapeDtypeStruct((n // tm, SSM_WIDTH * (tm // CHUNK), CHUNK), F32),
        ] + [jax.ShapeDtypeStruct(w.shape, BF16) for w in later],
        scratch_shapes=[pltpu.VMEM((RET_HEADS, HEAD_DIM, HEAD_DIM), F32),
                        pltpu.VMEM((D_MODEL, 4 * RET_WIDTH + SSM_WIDTH), BF16),
                        pltpu.VMEM((RET_WIDTH + SSM_WIDTH, D_MODEL), BF16)],
        compiler_params=pltpu.CompilerParams(
            dimension_semantics=("arbitrary", "arbitrary"), vmem_limit_bytes=VMEM_LIMIT_BYTES),
        name="mix_in",
    )(x2, g_pre, winf, cosf, sinf, cost, sint, xiq, ztt, gch, gn, *later)


def _s5_kernel(u_ref, pwk_ref, bc_ref, tab_ref, y_ref, sm_ref, wtap_ref, wst_ref, wct_ref, ktoe_ref, *, nc, nlev):
    m = u_ref.shape[0] * u_ref.shape[2]
    npair = SSM_GROUP // 2

    def smrow(base, i):
        return sm_ref[base + i:base + i + 1, :]

    def tabrow(base, i):
        return tab_ref[0, base + i:base + i + 1, :]

    def lo_half(shape):
        return lax.broadcasted_iota(jnp.int32, shape, 1) < HALF

    def swap(x):
        return pltpu.roll(x, HALF, 1)

    br0, bi0, cr0, ci0 = (bc_ref[0, k * SSM_GROUP:(k + 1) * SSM_GROUP, :] for k in range(4))
    sm_ref[ROW_BRR:ROW_BRR + SSM_GROUP, :] = br0 + swap(br0)
    sm_ref[ROW_BII:ROW_BII + SSM_GROUP, :] = bi0 + swap(bi0)
    sm_ref[ROW_BB1:ROW_BB1 + SSM_GROUP, :] = br0 - swap(bi0)
    sm_ref[ROW_BB2:ROW_BB2 + SSM_GROUP, :] = -bi0 - swap(br0)
    sm_ref[ROW_RC1:ROW_RC1 + SSM_GROUP, :] = cr0 + swap(cr0)
    sm_ref[ROW_RC2:ROW_RC2 + SSM_GROUP, :] = ci0 + swap(ci0)

    bb1 = sm_ref[ROW_BB1:ROW_BB1 + SSM_GROUP, :]
    bb2 = sm_ref[ROW_BB2:ROW_BB2 + SSM_GROUP, :]
    mc = lambda co: bb1 * smrow(ROW_RC1, co) + bb2 * smrow(ROW_RC2, co)
    mc_even = jnp.concatenate([mc(2 * cq) for cq in range(npair)], axis=0)
    mc_odd = jnp.concatenate([mc(2 * cq + 1) for cq in range(npair)], axis=0)
    pwk = pwk_ref[0]
    lo_sq = lo_half((2 * SSM_STATE, CHUNK))
    pw_lo = jnp.where(lo_sq, pwk, 0.0)
    pw_hi = jnp.where(lo_sq, 0.0, swap(pwk))
    wtap_ref[...] = _dot_3pass(mc_even, pw_lo) + _dot_3pass(mc_odd, pw_hi)

    sgn = jnp.where(lo_half((HALF, LANES)), -1.0, 1.0)
    pwt = pwk.T[:HALF, :]
    rev = (lax.broadcasted_iota(jnp.int32, (HALF, HALF), 0)
           + lax.broadcasted_iota(jnp.int32, (HALF, HALF), 1)) == HALF - 1
    a1 = _dot_3pass(jnp.where(rev, 1.0, 0.0), pwt)
    a2 = swap(a1) * sgn
    for ci in range(SSM_GROUP):
        wst_ref[ci * HALF:(ci + 1) * HALF, :] = (
            a1 * smrow(ROW_BRR, ci) + a2 * smrow(ROW_BII, ci)).astype(BF16)
    lam_row = pwt[1:2, :]
    lo_row = lo_half((1, LANES))
    lam_r = jnp.where(lo_row, lam_row, swap(lam_row))
    lam_i = jnp.where(lo_row, -swap(lam_row), lam_row)
    p1t = pwt * lam_r + swap(pwt) * lam_i
    b1 = p1t * (-sgn)
    b2 = -swap(p1t)
    for co in range(SSM_GROUP):
        wct_ref[co * HALF:(co + 1) * HALF, :] = (
            b1 * smrow(ROW_RC1, co) + b2 * smrow(ROW_RC2, co)).astype(BF16)

    u = [u_ref[:, ci].reshape(m, CHUNK) for ci in range(SSM_GROUP)]
    lo_m = lo_half((m, LANES))
    u_half = ([jnp.where(lo_m, u[2 * cp], swap(u[2 * cp + 1])) for cp in range(npair)],
              [jnp.where(lo_m, swap(u[2 * cp]), u[2 * cp + 1]) for cp in range(npair)])
    a = jnp.concatenate([jnp.concatenate([x.astype(BF16) for x in u_half[hf]], axis=1) for hf in range(2)], axis=0)
    st = _dot(a, wst_ref[...])
    s_even, s_odd = st[:m, :], st[m:, :]

    def cmul(x, xs, base, i):
        lr, li = tabrow(base, i), tabrow(base, i + 1)
        return x * lr + xs * li, xs * lr - x * li

    se_s = swap(s_even)
    st, sts = cmul(s_even, se_s, TAB_L64, 0)
    st, sts = st + s_odd, sts + swap(s_odd)
    block_id = lax.broadcasted_iota(jnp.int32, (m, LANES), 0) % nc
    for lev in range(nlev):
        k = 1 << lev
        keep = block_id >= k
        inc, incs = cmul(jnp.where(keep, pltpu.roll(st, k, 0), 0.0), jnp.where(keep, pltpu.roll(sts, k, 0), 0.0),
                         TAB_LAM, 2 * lev)
        st, sts = st + inc, sts + incs
    first = block_id >= 1
    xprev_even = jnp.where(first, pltpu.roll(st, 1, 0), 0.0)
    xprev_even_s = jnp.where(first, pltpu.roll(sts, 1, 0), 0.0)
    xprev_odd = cmul(xprev_even, xprev_even_s, TAB_L64, 0)[0] + s_even
    xprev = jnp.concatenate([xprev_even, xprev_odd], axis=0).astype(BF16)

    hrow = lax.broadcasted_iota(jnp.int32, (HALF, CHUNK), 0)
    hcol = lax.broadcasted_iota(jnp.int32, (HALF, CHUNK), 1) % HALF
    valid = hcol >= hrow
    nblk = MXU_DIM // CHUNK
    for j in range(npair // nblk):
        for cc in range(nblk):
            cq = j * nblk + cc
            for ci in range(SSM_GROUP):
                tile = jnp.broadcast_to(wtap_ref[cq * SSM_GROUP + ci:cq * SSM_GROUP + ci + 1, :], (HALF, CHUNK))
                tile = pltpu.roll(tile, 0, 1, stride=1, stride_axis=0)
                ktoe_ref[j, ci * HALF:(ci + 1) * HALF, cc * CHUNK:(cc + 1) * CHUNK] = (
                    jnp.where(valid, tile, 0.0).astype(BF16))
        yj = _dot(a, ktoe_ref[j]) + _dot_nt(xprev, wct_ref[j * MXU_DIM:(j + 1) * MXU_DIM, :])
        for cc in range(nblk):
            cq = j * nblk + cc
            for hf in range(2):
                yh = yj[hf * m:(hf + 1) * m, cc * CHUNK:(cc + 1) * CHUNK] + tabrow(TAB_DSK, cq) * u_half[hf][cq]
                y_ref[:, hf, cq] = yh.reshape(y_ref.shape[0], y_ref.shape[3], CHUNK)


def _s5(ut4, pwk, bc, tab, nc, nlev):
    nt, _, ncs, _ = ut4.shape
    npair = SSM_GROUP // 2
    grp = lambda g: (g, 0, 0)
    ugrp = lambda g: (0, g, 0, 0)
    return pl.pallas_call(
        functools.partial(_s5_kernel, nc=nc, nlev=nlev),
        grid=(SSM_GROUPS,),
        in_specs=[
            pl.BlockSpec((nt, SSM_GROUP, ncs, CHUNK), ugrp),
            pl.BlockSpec((1, 2 * SSM_STATE, CHUNK), grp),
            pl.BlockSpec((1, 4 * SSM_GROUP, LANES), grp),
            pl.BlockSpec((1, TAB_ROWS, LANES), grp),
        ],
        out_specs=pl.BlockSpec((nt, 2, npair, ncs, CHUNK), lambda g: (0, 0, g, 0, 0)),
        out_shape=jax.ShapeDtypeStruct((nt, 2, SSM_GROUPS * npair, ncs, CHUNK), F32),
        scratch_shapes=[
            pltpu.VMEM((SM_ROWS, LANES), F32),
            pltpu.VMEM((SSM_GROUP * SSM_GROUP // 2, CHUNK), F32),
            pltpu.VMEM((SSM_GROUP * HALF, 2 * SSM_STATE), BF16),
            pltpu.VMEM((SSM_GROUP * HALF, 2 * SSM_STATE), BF16),
            pltpu.VMEM((SSM_GROUP * HALF // MXU_DIM, SSM_GROUP * HALF, MXU_DIM), BF16),
        ],
        compiler_params=pltpu.CompilerParams(
            dimension_semantics=("parallel",), vmem_limit_bytes=VMEM_LIMIT_BYTES),
        name="s5",
    )(ut4, pwk, bc, tab)


def _mlp_kernel(x_ref, yret_ref, yc_ref, wglu_ref, wout_ref, gpost_ref, gpre2_ref, gpost2_ref,
                w1_ref, w2_ref, o_ref, *, ff_chunk, row_splits):
    tm = x_ref.shape[0]
    ncs = tm // CHUNK
    rs, cps = tm // row_splits, ncs // row_splits
    npairs = SSM_WIDTH // 2
    lo = lax.broadcasted_iota(jnp.int32, (npairs, LANES), 1) < HALF
    swap = lambda v: pltpu.roll(v, HALF, 1)

    def ssm_chunk(c):
        even = yc_ref[0, 0, pl.ds(c, npairs, stride=ncs), :]
        odd = yc_ref[0, 1, pl.ds(c, npairs, stride=ncs), :]
        y_e = jnp.where(lo, even, swap(odd))
        y_o = jnp.where(lo, swap(even), odd)
        return jnp.concatenate([jax.nn.gelu(y_e).astype(BF16), jax.nn.gelu(y_o).astype(BF16)], axis=0)

    mix_ret = [_dot(yret_ref[sp * rs:(sp + 1) * rs, :], wout_ref[:RET_WIDTH, :]) for sp in range(row_splits)]
    x1s, hs = [], []
    for sp in range(row_splits):
        rows = slice(sp * rs, (sp + 1) * rs)
        yst = jnp.concatenate([ssm_chunk(c) for c in range(sp * cps, (sp + 1) * cps)], axis=1)
        glu = _dot_tn(yst, wglu_ref[...])
        yssm = (glu[:, :SSM_WIDTH] * jax.nn.sigmoid(glu[:, SSM_WIDTH:])).astype(BF16)
        mix = mix_ret[sp] + _dot(yssm, wout_ref[RET_WIDTH:, :])
        x1 = x_ref[rows, :] + _rms(mix, gpost_ref[...])
        x1s.append(x1)
        hs.append(_rms(x1, gpre2_ref[...]).astype(BF16))
    mms = [None] * row_splits
    for c in range(D_FF // ff_chunk):
        for sp in range(row_splits):
            f = jnp.maximum(_dot(hs[sp], w1_ref[:, c * ff_chunk:(c + 1) * ff_chunk]), 0.0)
            part = _dot((f * f).astype(BF16), w2_ref[c * ff_chunk:(c + 1) * ff_chunk, :])
            mms[sp] = part if mms[sp] is None else mms[sp] + part
    for sp in range(row_splits):
        o_ref[sp * rs:(sp + 1) * rs, :] = x1s[sp] + _rms(mms[sp], gpost2_ref[...])


def _mlp(x2, yret, yc, wglu, wout, gpost, gpre2, gpost2, w1, w2, tm, ff_chunk):
    n = x2.shape[0]
    const = lambda i: (0, 0)
    once = pl.Buffered(1)
    return pl.pallas_call(
        functools.partial(_mlp_kernel, ff_chunk=ff_chunk, row_splits=MLP_ROW_SPLITS),
        grid=(n // tm,),
        in_specs=[
            pl.BlockSpec((tm, D_MODEL), lambda i: (i, 0)),
            pl.BlockSpec((tm, RET_WIDTH), lambda i: (i, 0)),
            pl.BlockSpec((1, 2, (SSM_WIDTH // 2) * (tm // CHUNK), CHUNK), lambda i: (i, 0, 0, 0)),
            pl.BlockSpec((SSM_WIDTH, 2 * SSM_WIDTH), const, pipeline_mode=once),
            pl.BlockSpec((D_MODEL, D_MODEL), const, pipeline_mode=once),
            pl.BlockSpec((1, D_MODEL), const),
            pl.BlockSpec((1, D_MODEL), const),
            pl.BlockSpec((1, D_MODEL), const),
            pl.BlockSpec((D_MODEL, D_FF), const, pipeline_mode=once),
            pl.BlockSpec((D_FF, D_MODEL), const, pipeline_mode=once),
        ],
        out_specs=pl.BlockSpec((tm, D_MODEL), lambda i: (i, 0)),
        out_shape=jax.ShapeDtypeStruct((n, D_MODEL), F32),
        compiler_params=pltpu.CompilerParams(
            dimension_semantics=("parallel",), vmem_limit_bytes=VMEM_LIMIT_BYTES),
        name="mlp",
    )(x2, yret, yc, wglu, wout, gpost, gpre2, gpost2, w1, w2)


def _rope_tables(seq_len):
    half = HEAD_DIM // 2
    inv_freq = ROPE_BASE ** (-np.arange(half, dtype=np.float64) / half)
    ang = np.arange(seq_len, dtype=np.float64)[:, None] * inv_freq[None, :]
    cos, sin = np.cos(ang), np.sin(ang)
    cosf = np.concatenate([cos, cos], axis=1)
    sinf = np.concatenate([-sin, sin], axis=1)
    kscale = HEAD_DIM ** -0.5
    f32 = lambda t: jnp.asarray(np.ascontiguousarray(t), dtype=F32)
    return f32(cosf), f32(sinf), f32(cosf.T * kscale), f32(sinf.T * kscale)


def _retention_tables(tm):
    log_gamma = np.log(1.0 - np.exp(np.linspace(math.log(1.0 / 32), math.log(1.0 / 512), RET_HEADS)))
    idx = np.arange(CHUNK, dtype=np.float64)
    xiq = np.exp((idx + 1.0 - CHUNK)[None, :] * log_gamma[:, None])
    zeta = np.exp((CHUNK - 1 - idx)[None, :] * log_gamma[:, None])
    gch = np.exp(CHUNK * log_gamma)
    xiq_b = np.broadcast_to(xiq[:, :, None], (RET_HEADS, CHUNK, HEAD_DIM))
    ztt = np.tile(zeta, (1, tm // CHUNK))[:, None, :]
    gch_b = np.broadcast_to(gch[:, None, None], (RET_HEADS, 1, HEAD_DIM))
    f32 = lambda t: jnp.asarray(np.ascontiguousarray(t), dtype=F32)
    return f32(xiq_b), f32(ztt), f32(gch_b)


def _s5_tables(lam_re, lam_im, log_dt, b_re, b_im, c_re, c_im, d_skip, nlev):
    assert nlev <= MAX_SCAN_LEVELS
    a = jnp.minimum(lam_re.astype(F32), -1e-4)
    b = lam_im.astype(F32)
    dt = jnp.exp(log_dt.astype(F32))[:, None]
    tau = jnp.arange(CHUNK, dtype=F32)
    mag = jnp.exp((a * dt)[:, :, None] * tau)
    ph = (b * dt)[:, :, None] * tau
    pr, pi = mag * jnp.cos(ph), mag * jnp.sin(ph)
    lr1, li = pr[:, :, 1] - 1.0, pi[:, :, 1]
    den = a * a + b * b
    c0r, c0i = (lr1 * a + li * b) / den, (li * a - lr1 * b) / den
    bre, bim = b_re.astype(F32), b_im.astype(F32)
    br = jnp.swapaxes(c0r[:, :, None] * bre - c0i[:, :, None] * bim, 1, 2)
    bi = jnp.swapaxes(c0r[:, :, None] * bim + c0i[:, :, None] * bre, 1, 2)
    cr, ci = c_re.astype(F32), c_im.astype(F32)

    pwk = jnp.concatenate([pr, pi], axis=1)
    cat = lambda x, y: jnp.concatenate([x, y], axis=-1)

    lam_rows = []
    for step in [float(CHUNK * (1 << lev)) for lev in range(MAX_SCAN_LEVELS)] + [float(HALF)]:
        mg = jnp.exp(a * dt * step)
        re, im = mg * jnp.cos(b * dt * step), mg * jnp.sin(b * dt * step)
        lam_rows += [cat(re, re), cat(-im, im)]
    dsk = jnp.repeat(d_skip.astype(F32).reshape(SSM_GROUPS, SSM_GROUP // 2, 2), HALF, axis=2)
    tab = jnp.concatenate([jnp.stack(lam_rows, axis=1), dsk], axis=1)
    bc = jnp.concatenate([br, bi, cr, ci], axis=1)
    bc = jnp.pad(bc, ((0, 0), (0, 0), (0, LANES - SSM_STATE)))
    return pwk, bc, tab


def _tiles(seq_len):
    tm_mix = min(1024, seq_len)
    tm_mlp = tm_mix
    ff_chunk = 1024
    assert seq_len % CHUNK == 0 and seq_len % tm_mix == 0
    return tm_mix, tm_mlp, ff_chunk


def kernel(x, norm_mix_pre, norm_mix_post, w_in, ret_gn_gain, ssm_lambda_re, ssm_lambda_im, ssm_log_dt,
           ssm_b_re, ssm_b_im, ssm_c_re, ssm_c_im, ssm_d, w_glu, w_out, norm_mlp_pre, norm_mlp_post,
           w_ff1, w_ff2):
    batch, seq_len, _ = x.shape
    depth = w_in.shape[0]
    n = batch * seq_len
    nc = seq_len // CHUNK
    nlev = max(1, (nc - 1).bit_length())
    tm_mix, tm_mlp, ff_chunk = _tiles(seq_len)

    cosf, sinf, cost, sint = _rope_tables(seq_len)
    xiq, ztt, gch = _retention_tables(tm_mix)
    x2 = x.reshape(n, D_MODEL)
    for i in range(depth):
        gn = ret_gn_gain[i].astype(F32).reshape(RET_HEADS, 1, HEAD_DIM)
        wglu_eo = w_glu[i].reshape(SSM_WIDTH // 2, 2, 2 * SSM_WIDTH).swapaxes(0, 1).reshape(SSM_WIDTH, 2 * SSM_WIDTH)
        yret, ut, wglu, wout, w1, w2 = _mix_in(
            x2, norm_mix_pre[i][None, :], w_in[i], cosf, sinf, cost, sint, xiq, ztt, gch, gn,
            wglu_eo, w_out[i], w_ff1[i], w_ff2[i], batch, seq_len, tm_mix)
        pwk, bc, tab = _s5_tables(
            ssm_lambda_re[i], ssm_lambda_im[i], ssm_log_dt[i], ssm_b_re[i], ssm_b_im[i],
            ssm_c_re[i], ssm_c_im[i], ssm_d[i], nlev)
        ncs = tm_mix // CHUNK
        yc = _s5(ut.reshape(n // tm_mix, SSM_WIDTH, ncs, CHUNK), pwk, bc, tab, nc, nlev)
        yc = yc.reshape(n // tm_mlp, 2, (SSM_WIDTH // 2) * ncs, CHUNK)
        x2 = _mlp(x2, yret, yc, wglu, wout,
                  norm_mix_post[i][None, :], norm_mlp_pre[i][None, :], norm_mlp_post[i][None, :],
                  w1, w2, tm_mlp, ff_chunk)
    return x2.reshape(batch, seq_len, D_MODEL)
```

```python
import functools
import math

import jax
import jax.numpy as jnp
import numpy as np
from jax import lax
from jax.experimental import pallas as pl
from jax.experimental.pallas import tpu as pltpu

D_MODEL = 1024
RET_WIDTH = 512
RET_HEADS = 4
HEAD_DIM = 128
CHUNK = 128
HALF = CHUNK // 2
ROPE_BASE = 10000.0
SSM_WIDTH = 512
SSM_GROUP = 16
SSM_GROUPS = 32
SSM_STATE = 64
D_FF = 4096
NORM_EPS = 1e-6
LANES = 128
MXU_DIM = 256
BF16_SUBLANES = 16
MLP_ROW_SPLITS = 2

BF16 = jnp.bfloat16
F32 = jnp.float32

VMEM_LIMIT_BYTES = 56 * 1024 * 1024

ROW_BRR, ROW_BII, ROW_BB1, ROW_BB2, ROW_RC1, ROW_RC2, SM_ROWS = 0, 16, 32, 48, 64, 80, 96
TAB_LAM, TAB_L64, TAB_DSK, TAB_ROWS = 0, 14, 16, 32
MAX_SCAN_LEVELS = (TAB_L64 - TAB_LAM) // 2


def _dot(a, b):
    return jnp.dot(a, b, preferred_element_type=F32)


def _dot_nt(a, b):
    return lax.dot_general(a, b, (((1,), (1,)), ((), ())), preferred_element_type=F32)


def _dot_tn(a, b):
    return lax.dot_general(a, b, (((0,), (0,)), ((), ())), preferred_element_type=F32)


def _dot_3pass(a, b):
    a_hi = a.astype(BF16)
    a_lo = (a - a_hi.astype(F32)).astype(BF16)
    b_hi = b.astype(BF16)
    b_lo = (b - b_hi.astype(F32)).astype(BF16)
    return _dot(a_hi, b_hi) + (_dot(a_hi, b_lo) + _dot(a_lo, b_hi))


def _rms(x, g):
    ms = jnp.mean(x * x, axis=-1, keepdims=True)
    return x * lax.rsqrt(ms + NORM_EPS) * g


def _mix_in_kernel(x_ref, g_ref, winf_ref, cosf_ref, sinf_ref, cost_ref, sint_ref,
                   xiq_ref, ztt_ref, gch_ref, gn_ref, wglu_ref, wout_ref, w1_ref, w2_ref,
                   yret_ref, ut_ref, wglub_ref, woutb_ref, w1b_ref, w2b_ref, r_ref, win_ref, wkut_ref):
    @pl.when((pl.program_id(0) == 0) & (pl.program_id(1) == 0))
    def _():
        for c0 in range(0, win_ref.shape[1], RET_WIDTH):
            win_ref[:, c0:c0 + RET_WIDTH] = winf_ref[:, c0:c0 + RET_WIDTH].astype(BF16)
        for blk in range((RET_WIDTH + SSM_WIDTH) // LANES):
            c0 = RET_WIDTH + blk * LANES if blk * LANES < RET_WIDTH else 3 * RET_WIDTH + blk * LANES
            wkut_ref[blk * LANES:(blk + 1) * LANES, :] = winf_ref[:, c0:c0 + LANES].T.astype(BF16)

    wglub_ref[...] = wglu_ref[...].astype(BF16)
    woutb_ref[...] = wout_ref[...].astype(BF16)
    w1b_ref[...] = w1_ref[...].astype(BF16)
    w2b_ref[...] = w2_ref[...].astype(BF16)

    @pl.when(pl.program_id(1) == 0)
    def _():
        r_ref[...] = jnp.zeros_like(r_ref)

    tm = x_ref.shape[0]
    half = HEAD_DIM // 2
    h = _rms(x_ref[...], g_ref[...]).astype(BF16)
    pq = _dot(h, win_ref[:, :RET_WIDTH])
    pvg = _dot(h, win_ref[:, 2 * RET_WIDTH:4 * RET_WIDTH])
    pt = _dot_nt(wkut_ref[...], h)
    ncs = tm // CHUNK
    for c in range(ncs):
        ut_ref[0, pl.ds(c, SSM_WIDTH, stride=ncs), :] = pt[RET_WIDTH:, c * CHUNK:(c + 1) * CHUNK]

    row = lax.broadcasted_iota(jnp.int32, (CHUNK, CHUNK), 0)
    col = lax.broadcasted_iota(jnp.int32, (CHUNK, CHUNK), 1)
    causal = row >= col
    cosf, sinf = cosf_ref[...], sinf_ref[...]
    cost, sint = cost_ref[...], sint_ref[...]
    for hd in range(RET_HEADS):
        sl = slice(hd * HEAD_DIM, (hd + 1) * HEAD_DIM)
        qh = pq[:, sl]
        qh = qh * cosf + pltpu.roll(qh, half, 1) * sinf
        kh = pt[sl, :]
        sw = jnp.concatenate([kh[half:, :], kh[:half, :]], axis=0)
        kh = ((kh * cost + sw * sint) * ztt_ref[hd]).astype(BF16)
        g_state = r_ref[hd]
        for c in range(tm // CHUNK):
            rows = slice(c * CHUNK, (c + 1) * CHUNK)
            qc = (qh[rows, :] * xiq_ref[hd]).astype(BF16)
            kc = kh[:, rows]
            vc = pvg[rows, sl].astype(BF16)
            s = jnp.where(causal, _dot(qc, kc), 0.0).astype(BF16)
            y = _dot(jnp.concatenate([s, qc], axis=1), jnp.concatenate([vc, g_state.astype(BF16)], axis=0))
            g_state = gch_ref[hd] * (g_state + _dot(kc, vc))
            mu = jnp.mean(y, axis=-1, keepdims=True)
            yc = y - mu
            var = jnp.mean(yc * yc, axis=-1, keepdims=True)
            yn = yc * lax.rsqrt(var + NORM_EPS) * gn_ref[hd]
            gate = pvg[rows, RET_WIDTH + hd * HEAD_DIM:RET_WIDTH + (hd + 1) * HEAD_DIM]
            yret_ref[rows, sl] = (jax.nn.silu(gate) * yn).astype(BF16)
        r_ref[hd] = g_state


def _mix_in(x2, g_pre, winf, cosf, sinf, cost, sint, xiq, ztt, gch, gn, wglu, wout, w1, w2, batch, seq_len, tm):
    n = x2.shape[0]
    nseq = seq_len // tm
    nsteps = batch * nseq
    tok = lambda b, j: (b * nseq + j, 0)
    slab = lambda w: pl.BlockSpec((w.shape[0] // nsteps, w.shape[1]), tok)
    later = (wglu, wout, w1, w2)
    assert all(w.shape[0] % (BF16_SUBLANES * nsteps) == 0 for w in later)
    tile3 = lambda b, j: (b * nseq + j, 0, 0)
    const2 = lambda b, j: (0, 0)
    const3 = lambda b, j: (0, 0, 0)
    return pl.pallas_call(
        _mix_in_kernel,
        grid=(batch, nseq),
        in_specs=[
            pl.BlockSpec((tm, D_MODEL), tok),
            pl.BlockSpec((1, D_MODEL), const2),
            pl.BlockSpec((D_MODEL, 4 * RET_WIDTH + SSM_WIDTH), const2, pipeline_mode=pl.Buffered(1)),
            pl.BlockSpec((tm, HEAD_DIM), lambda b, j: (j, 0)),
            pl.BlockSpec((tm, HEAD_DIM), lambda b, j: (j, 0)),
            pl.BlockSpec((HEAD_DIM, tm), lambda b, j: (0, j)),
            pl.BlockSpec((HEAD_DIM, tm), lambda b, j: (0, j)),
            pl.BlockSpec((RET_HEADS, CHUNK, HEAD_DIM), const3),
            pl.BlockSpec((RET_HEADS, 1, tm), const3),
            pl.BlockSpec((RET_HEADS, 1, HEAD_DIM), const3),
            pl.BlockSpec((RET_HEADS, 1, HEAD_DIM), const3),
        ] + [slab(w) for w in later],
        out_specs=[
            pl.BlockSpec((tm, RET_WIDTH), tok),
            pl.BlockSpec((1, SSM_WIDTH * (tm // CHUNK), CHUNK), tile3),
        ] + [slab(w) for w in later],
        out_shape=[
            jax.ShapeDtypeStruct((n, RET_WIDTH), BF16),
            jax.ShapeDtypeStruct((n // tm, SSM_WIDTH * (tm // CHUNK), CHUNK), F32),
        ] + [jax.ShapeDtypeStruct(w.shape, BF16) for w in later],
        scratch_shapes=[pltpu.VMEM((RET_HEADS, HEAD_DIM, HEAD_DIM), F32),
                        pltpu.VMEM((D_MODEL, 4 * RET_WIDTH + SSM_WIDTH), BF16),
                        pltpu.VMEM((RET_WIDTH + SSM_WIDTH, D_MODEL), BF16)],
        compiler_params=pltpu.CompilerParams(
            dimension_semantics=("arbitrary", "arbitrary"), vmem_limit_bytes=VMEM_LIMIT_BYTES),
        name="mix_in",
    )(x2, g_pre, winf, cosf, sinf, cost, sint, xiq, ztt, gch, gn, *later)


def _s5_kernel(u_ref, pwk_ref, bc_ref, tab_ref, y_ref, sm_ref, wtap_ref, wst_ref, wct_ref, ktoe_ref, *, nc, nlev):
    m = u_ref.shape[0] * u_ref.shape[2]
    npair = SSM_GROUP // 2

    def smrow(base, i):
        return sm_ref[base + i:base + i + 1, :]

    def tabrow(base, i):
        return tab_ref[0, base + i:base + i + 1, :]

    def lo_half(shape):
        return lax.broadcasted_iota(jnp.int32, shape, 1) < HALF

    def swap(x):
        return pltpu.roll(x, HALF, 1)

    br0, bi0, cr0, ci0 = (bc_ref[0, k * SSM_GROUP:(k + 1) * SSM_GROUP, :] for k in range(4))
    sm_ref[ROW_BRR:ROW_BRR + SSM_GROUP, :] = br0 + swap(br0)
    sm_ref[ROW_BII:ROW_BII + SSM_GROUP, :] = bi0 + swap(bi0)
    sm_ref[ROW_BB1:ROW_BB1 + SSM_GROUP, :] = br0 - swap(bi0)
    sm_ref[ROW_BB2:ROW_BB2 + SSM_GROUP, :] = -bi0 - swap(br0)
    sm_ref[ROW_RC1:ROW_RC1 + SSM_GROUP, :] = cr0 + swap(cr0)
    sm_ref[ROW_RC2:ROW_RC2 + SSM_GROUP, :] = ci0 + swap(ci0)

    bb1 = sm_ref[ROW_BB1:ROW_BB1 + SSM_GROUP, :]
    bb2 = sm_ref[ROW_BB2:ROW_BB2 + SSM_GROUP, :]
    mc = lambda co: bb1 * smrow(ROW_RC1, co) + bb2 * smrow(ROW_RC2, co)
    mc_even = jnp.concatenate([mc(2 * cq) for cq in range(npair)], axis=0)
    mc_odd = jnp.concatenate([mc(2 * cq + 1) for cq in range(npair)], axis=0)
    pwk = pwk_ref[0]
    lo_sq = lo_half((2 * SSM_STATE, CHUNK))
    pw_lo = jnp.where(lo_sq, pwk, 0.0)
    pw_hi = jnp.where(lo_sq, 0.0, swap(pwk))
    wtap_ref[...] = _dot_3pass(mc_even, pw_lo) + _dot_3pass(mc_odd, pw_hi)

    sgn = jnp.where(lo_half((HALF, LANES)), -1.0, 1.0)
    pwt = pwk.T[:HALF, :]
    rev = (lax.broadcasted_iota(jnp.int32, (HALF, HALF), 0)
           + lax.broadcasted_iota(jnp.int32, (HALF, HALF), 1)) == HALF - 1
    a1 = _dot_3pass(jnp.where(rev, 1.0, 0.0), pwt)
    a2 = swap(a1) * sgn
    for ci in range(SSM_GROUP):
        wst_ref[ci * HALF:(ci + 1) * HALF, :] = (
            a1 * smrow(ROW_BRR, ci) + a2 * smrow(ROW_BII, ci)).astype(BF16)
    lam_row = pwt[1:2, :]
    lo_row = lo_half((1, LANES))
    lam_r = jnp.where(lo_row, lam_row, swap(lam_row))
    lam_i = jnp.where(lo_row, -swap(lam_row), lam_row)
    p1t = pwt * lam_r + swap(pwt) * lam_i
    b1 = p1t * (-sgn)
    b2 = -swap(p1t)
    for co in range(SSM_GROUP):
        wct_ref[co * HALF:(co + 1) * HALF, :] = (
            b1 * smrow(ROW_RC1, co) + b2 * smrow(ROW_RC2, co)).astype(BF16)

    u = [u_ref[:, ci].reshape(m, CHUNK) for ci in range(SSM_GROUP)]
    lo_m = lo_half((m, LANES))
    a_even = [jnp.where(lo_m, u[2 * cp], swap(u[2 * cp + 1])).astype(BF16) for cp in range(npair)]
    a_odd = [jnp.where(lo_m, swap(u[2 * cp]), u[2 * cp + 1]).astype(BF16) for cp in range(npair)]
    a = jnp.concatenate([jnp.concatenate(a_even, axis=1), jnp.concatenate(a_odd, axis=1)], axis=0)
    st = _dot(a, wst_ref[...])
    s_even, s_odd = st[:m, :], st[m:, :]

    def cmul(x, xs, base, i):
        lr, li = tabrow(base, i), tabrow(base, i + 1)
        return x * lr + xs * li, xs * lr - x * li

    se_s = swap(s_even)
    st, sts = cmul(s_even, se_s, TAB_L64, 0)
    st, sts = st + s_odd, sts + swap(s_odd)
    block_id = lax.broadcasted_iota(jnp.int32, (m, LANES), 0) % nc
    for lev in range(nlev):
        k = 1 << lev
        keep = block_id >= k
        inc, incs = cmul(jnp.where(keep, pltpu.roll(st, k, 0), 0.0), jnp.where(keep, pltpu.roll(sts, k, 0), 0.0),
                         TAB_LAM, 2 * lev)
        st, sts = st + inc, sts + incs
    first = block_id >= 1
    xprev_even = jnp.where(first, pltpu.roll(st, 1, 0), 0.0)
    xprev_even_s = jnp.where(first, pltpu.roll(sts, 1, 0), 0.0)
    xprev_odd = cmul(xprev_even, xprev_even_s, TAB_L64, 0)[0] + s_even
    xprev = jnp.concatenate([xprev_even, xprev_odd], axis=0).astype(BF16)

    hrow = lax.broadcasted_iota(jnp.int32, (HALF, CHUNK), 0)
    hcol = lax.broadcasted_iota(jnp.int32, (HALF, CHUNK), 1) % HALF
    valid = hcol >= hrow
    nblk = MXU_DIM // CHUNK
    for j in range(npair // nblk):
        for cc in range(nblk):
            cq = j * nblk + cc
            for ci in range(SSM_GROUP):
                tile = jnp.broadcast_to(wtap_ref[cq * SSM_GROUP + ci:cq * SSM_GROUP + ci + 1, :], (HALF, CHUNK))
                tile = pltpu.roll(tile, 0, 1, stride=1, stride_axis=0)
                ktoe_ref[j, ci * HALF:(ci + 1) * HALF, cc * CHUNK:(cc + 1) * CHUNK] = (
                    jnp.where(valid, tile, 0.0).astype(BF16))
        yj = _dot(a, ktoe_ref[j]) + _dot_nt(xprev, wct_ref[j * MXU_DIM:(j + 1) * MXU_DIM, :])
        for cc in range(nblk):
            cq = j * nblk + cc
            y_even = yj[:m, cc * CHUNK:(cc + 1) * CHUNK]
            y_odd = yj[m:, cc * CHUNK:(cc + 1) * CHUNK]
            outs = (jnp.where(lo_m, y_even, swap(y_odd)), jnp.where(lo_m, swap(y_even), y_odd))
            for i in range(2):
                co = 2 * cq + i
                yc = outs[i] + tabrow(TAB_DSK, co) * u[co]
                y_ref[:, co] = jax.nn.gelu(yc).reshape(y_ref.shape[0], y_ref.shape[2], CHUNK)


def _s5(ut4, pwk, bc, tab, nc, nlev):
    nt, _, ncs, _ = ut4.shape
    grp = lambda g: (g, 0, 0)
    ugrp = lambda g: (0, g, 0, 0)
    return pl.pallas_call(
        functools.partial(_s5_kernel, nc=nc, nlev=nlev),
        grid=(SSM_GROUPS,),
        in_specs=[
            pl.BlockSpec((nt, SSM_GROUP, ncs, CHUNK), ugrp),
            pl.BlockSpec((1, 2 * SSM_STATE, CHUNK), grp),
            pl.BlockSpec((1, 4 * SSM_GROUP, LANES), grp),
            pl.BlockSpec((1, TAB_ROWS, LANES), grp),
        ],
        out_specs=pl.BlockSpec((nt, SSM_GROUP, ncs, CHUNK), ugrp),
        out_shape=jax.ShapeDtypeStruct(ut4.shape, F32),
        scratch_shapes=[
            pltpu.VMEM((SM_ROWS, LANES), F32),
            pltpu.VMEM((SSM_GROUP * SSM_GROUP // 2, CHUNK), F32),
            pltpu.VMEM((SSM_GROUP * HALF, 2 * SSM_STATE), BF16),
            pltpu.VMEM((SSM_GROUP * HALF, 2 * SSM_STATE), BF16),
            pltpu.VMEM((SSM_GROUP * HALF // MXU_DIM, SSM_GROUP * HALF, MXU_DIM), BF16),
        ],
        compiler_params=pltpu.CompilerParams(
            dimension_semantics=("parallel",), vmem_limit_bytes=VMEM_LIMIT_BYTES),
        name="s5",
    )(ut4, pwk, bc, tab)


def _mlp_kernel(x_ref, yret_ref, yst_ref, wglu_ref, wout_ref, gpost_ref, gpre2_ref, gpost2_ref,
                w1_ref, w2_ref, o_ref, *, ff_chunk, row_splits):
    tm = x_ref.shape[0]
    ncs = tm // CHUNK
    rs, cps = tm // row_splits, ncs // row_splits
    x1s, hs = [], []
    for sp in range(row_splits):
        rows = slice(sp * rs, (sp + 1) * rs)
        yst = jnp.concatenate([yst_ref[0, pl.ds(c, SSM_WIDTH, stride=ncs), :].astype(BF16)
                               for c in range(sp * cps, (sp + 1) * cps)], axis=1)
        glu = _dot_tn(yst, wglu_ref[...])
        yssm = (glu[:, :SSM_WIDTH] * jax.nn.sigmoid(glu[:, SSM_WIDTH:])).astype(BF16)
        mix = _dot(yret_ref[rows, :], wout_ref[:RET_WIDTH, :]) + _dot(yssm, wout_ref[RET_WIDTH:, :])
        x1 = x_ref[rows, :] + _rms(mix, gpost_ref[...])
        x1s.append(x1)
        hs.append(_rms(x1, gpre2_ref[...]).astype(BF16))
    mms = [None] * row_splits
    for c in range(D_FF // ff_chunk):
        for sp in range(row_splits):
            f = jnp.maximum(_dot(hs[sp], w1_ref[:, c * ff_chunk:(c + 1) * ff_chunk]), 0.0)
            part = _dot((f * f).astype(BF16), w2_ref[c * ff_chunk:(c + 1) * ff_chunk, :])
            mms[sp] = part if mms[sp] is None else mms[sp] + part
    for sp in range(row_splits):
        o_ref[sp * rs:(sp + 1) * rs, :] = x1s[sp] + _rms(mms[sp], gpost2_ref[...])


def _mlp(x2, yret, yst, wglu, wout, gpost, gpre2, gpost2, w1, w2, tm, ff_chunk):
    n = x2.shape[0]
    const = lambda i: (0, 0)
    once = pl.Buffered(1)
    return pl.pallas_call(
        functools.partial(_mlp_kernel, ff_chunk=ff_chunk, row_splits=MLP_ROW_SPLITS),
        grid=(n // tm,),
        in_specs=[
            pl.BlockSpec((tm, D_MODEL), lambda i: (i, 0)),
            pl.BlockSpec((tm, RET_WIDTH), lambda i: (i, 0)),
            pl.BlockSpec((1, SSM_WIDTH * (tm // CHUNK), CHUNK), lambda i: (i, 0, 0)),
            pl.BlockSpec((SSM_WIDTH, 2 * SSM_WIDTH), const, pipeline_mode=once),
            pl.BlockSpec((D_MODEL, D_MODEL), const, pipeline_mode=once),
            pl.BlockSpec((1, D_MODEL), const),
            pl.BlockSpec((1, D_MODEL), const),
            pl.BlockSpec((1, D_MODEL), const),
            pl.BlockSpec((D_MODEL, D_FF), const, pipeline_mode=once),
            pl.BlockSpec((D_FF, D_MODEL), const, pipeline_mode=once),
        ],
        out_specs=pl.BlockSpec((tm, D_MODEL), lambda i: (i, 0)),
        out_shape=jax.ShapeDtypeStruct((n, D_MODEL), F32),
        compiler_params=pltpu.CompilerParams(
            dimension_semantics=("parallel",), vmem_limit_bytes=VMEM_LIMIT_BYTES),
        name="mlp",
    )(x2, yret, yst, wglu, wout, gpost, gpre2, gpost2, w1, w2)


def _rope_tables(seq_len):
    half = HEAD_DIM // 2
    inv_freq = ROPE_BASE ** (-np.arange(half, dtype=np.float64) / half)
    ang = np.arange(seq_len, dtype=np.float64)[:, None] * inv_freq[None, :]
    cos, sin = np.cos(ang), np.sin(ang)
    cosf = np.concatenate([cos, cos], axis=1)
    sinf = np.concatenate([-sin, sin], axis=1)
    kscale = HEAD_DIM ** -0.5
    f32 = lambda t: jnp.asarray(np.ascontiguousarray(t), dtype=F32)
    return f32(cosf), f32(sinf), f32(cosf.T * kscale), f32(sinf.T * kscale)


def _retention_tables(tm):
    log_gamma = np.log(1.0 - np.exp(np.linspace(math.log(1.0 / 32), math.log(1.0 / 512), RET_HEADS)))
    idx = np.arange(CHUNK, dtype=np.float64)
    xiq = np.exp((idx + 1.0 - CHUNK)[None, :] * log_gamma[:, None])
    zeta = np.exp((CHUNK - 1 - idx)[None, :] * log_gamma[:, None])
    gch = np.exp(CHUNK * log_gamma)
    xiq_b = np.broadcast_to(xiq[:, :, None], (RET_HEADS, CHUNK, HEAD_DIM))
    ztt = np.tile(zeta, (1, tm // CHUNK))[:, None, :]
    gch_b = np.broadcast_to(gch[:, None, None], (RET_HEADS, 1, HEAD_DIM))
    f32 = lambda t: jnp.asarray(np.ascontiguousarray(t), dtype=F32)
    return f32(xiq_b), f32(ztt), f32(gch_b)


def _s5_tables(lam_re, lam_im, log_dt, b_re, b_im, c_re, c_im, d_skip, nlev):
    assert nlev <= MAX_SCAN_LEVELS
    a = jnp.minimum(lam_re.astype(F32), -1e-4)
    b = lam_im.astype(F32)
    dt = jnp.exp(log_dt.astype(F32))[:, None]
    tau = jnp.arange(CHUNK, dtype=F32)
    mag = jnp.exp((a * dt)[:, :, None] * tau)
    ph = (b * dt)[:, :, None] * tau
    pr, pi = mag * jnp.cos(ph), mag * jnp.sin(ph)
    lr1, li = pr[:, :, 1] - 1.0, pi[:, :, 1]
    den = a * a + b * b
    c0r, c0i = (lr1 * a + li * b) / den, (li * a - lr1 * b) / den
    bre, bim = b_re.astype(F32), b_im.astype(F32)
    br = jnp.swapaxes(c0r[:, :, None] * bre - c0i[:, :, None] * bim, 1, 2)
    bi = jnp.swapaxes(c0r[:, :, None] * bim + c0i[:, :, None] * bre, 1, 2)
    cr, ci = c_re.astype(F32), c_im.astype(F32)

    pwk = jnp.concatenate([pr, pi], axis=1)
    cat = lambda x, y: jnp.concatenate([x, y], axis=-1)

    lam_rows = []
    for step in [float(CHUNK * (1 << lev)) for lev in range(MAX_SCAN_LEVELS)] + [float(HALF)]:
        mg = jnp.exp(a * dt * step)
        re, im = mg * jnp.cos(b * dt * step), mg * jnp.sin(b * dt * step)
        lam_rows += [cat(re, re), cat(-im, im)]
    dsk = jnp.broadcast_to(d_skip.astype(F32).reshape(SSM_GROUPS, SSM_GROUP, 1), (SSM_GROUPS, SSM_GROUP, LANES))
    tab = jnp.concatenate([jnp.stack(lam_rows, axis=1), dsk], axis=1)
    bc = jnp.concatenate([br, bi, cr, ci], axis=1)
    bc = jnp.pad(bc, ((0, 0), (0, 0), (0, LANES - SSM_STATE)))
    return pwk, bc, tab


def _tiles(seq_len):
    tm_mix = min(1024, seq_len)
    tm_mlp = tm_mix
    ff_chunk = 1024
    assert seq_len % CHUNK == 0 and seq_len % tm_mix == 0
    return tm_mix, tm_mlp, ff_chunk


def kernel(x, norm_mix_pre, norm_mix_post, w_in, ret_gn_gain, ssm_lambda_re, ssm_lambda_im, ssm_log_dt,
           ssm_b_re, ssm_b_im, ssm_c_re, ssm_c_im, ssm_d, w_glu, w_out, norm_mlp_pre, norm_mlp_post,
           w_ff1, w_ff2):
    batch, seq_len, _ = x.shape
    depth = w_in.shape[0]
    n = batch * seq_len
    nc = seq_len // CHUNK
    nlev = max(1, (nc - 1).bit_length())
    tm_mix, tm_mlp, ff_chunk = _tiles(seq_len)

    cosf, sinf, cost, sint = _rope_tables(seq_len)
    xiq, ztt, gch = _retention_tables(tm_mix)
    x2 = x.reshape(n, D_MODEL)
    for i in range(depth):
        gn = ret_gn_gain[i].astype(F32).reshape(RET_HEADS, 1, HEAD_DIM)
        yret, ut, wglu, wout, w1, w2 = _mix_in(
            x2, norm_mix_pre[i][None, :], w_in[i], cosf, sinf, cost, sint, xiq, ztt, gch, gn,
            w_glu[i], w_out[i], w_ff1[i], w_ff2[i], batch, seq_len, tm_mix)
        pwk, bc, tab = _s5_tables(
            ssm_lambda_re[i], ssm_lambda_im[i], ssm_log_dt[i], ssm_b_re[i], ssm_b_im[i],
            ssm_c_re[i], ssm_c_im[i], ssm_d[i], nlev)
        ncs = tm_mix // CHUNK
        yst = _s5(ut.reshape(n // tm_mix, SSM_WIDTH, ncs, CHUNK), pwk, bc, tab, nc, nlev)
        yst = yst.reshape(n // tm_mlp, SSM_WIDTH * ncs, CHUNK)
        x2 = _mlp(x2, yret, yst, wglu, wout,
                  norm_mix_post[i][None, :], norm_mlp_pre[i][None, :], norm_mlp_post[i][None, :],
                  w1, w2, tm_mlp, ff_chunk)
    return x2.reshape(batch, seq_len, D_MODEL)
```

```python
import functools
import math

import jax
import jax.numpy as jnp
import numpy as np
from jax import lax
from jax.experimental import pallas as pl
from jax.experimental.pallas import tpu as pltpu

D_MODEL = 1024
RET_WIDTH = 512
RET_HEADS = 4
HEAD_DIM = 128
CHUNK = 128
HALF = CHUNK // 2
ROPE_BASE = 10000.0
SSM_WIDTH = 512
SSM_GROUP = 16
SSM_GROUPS = 32
SSM_STATE = 64
D_FF = 4096
NORM_EPS = 1e-6
LANES = 128
MXU_DIM = 256
BF16_SUBLANES = 16
MLP_ROW_SPLITS = 2

BF16 = jnp.bfloat16
F32 = jnp.float32

VMEM_LIMIT_BYTES = 56 * 1024 * 1024

ROW_BRR, ROW_BII, ROW_BB1, ROW_BB2, ROW_RC1, ROW_RC2, SM_ROWS = 0, 16, 32, 48, 64, 80, 96
TAB_LAM, TAB_L64, TAB_DSK, TAB_ROWS = 0, 14, 16, 32
MAX_SCAN_LEVELS = (TAB_L64 - TAB_LAM) // 2


def _dot(a, b):
    return jnp.dot(a, b, preferred_element_type=F32)


def _dot_nt(a, b):
    return lax.dot_general(a, b, (((1,), (1,)), ((), ())), preferred_element_type=F32)


def _dot_tn(a, b):
    return lax.dot_general(a, b, (((0,), (0,)), ((), ())), preferred_element_type=F32)


def _dot_3pass(a, b):
    a_hi = a.astype(BF16)
    a_lo = (a - a_hi.astype(F32)).astype(BF16)
    b_hi = b.astype(BF16)
    b_lo = (b - b_hi.astype(F32)).astype(BF16)
    return _dot(a_hi, b_hi) + (_dot(a_hi, b_lo) + _dot(a_lo, b_hi))


def _rms(x, g):
    ms = jnp.mean(x * x, axis=-1, keepdims=True)
    return x * lax.rsqrt(ms + NORM_EPS) * g


def _mix_in_kernel(x_ref, g_ref, winf_ref, cosf_ref, sinf_ref, cost_ref, sint_ref,
                   xiq_ref, ztt_ref, gch_ref, gn_ref, wglu_ref, wout_ref, w1_ref, w2_ref,
                   yret_ref, ut_ref, wglub_ref, woutb_ref, w1b_ref, w2b_ref, r_ref, win_ref, wkut_ref):
    @pl.when((pl.program_id(0) == 0) & (pl.program_id(1) == 0))
    def _():
        for c0 in range(0, win_ref.shape[1], RET_WIDTH):
            win_ref[:, c0:c0 + RET_WIDTH] = winf_ref[:, c0:c0 + RET_WIDTH].astype(BF16)
        for blk in range((RET_WIDTH + SSM_WIDTH) // LANES):
            c0 = RET_WIDTH + blk * LANES if blk * LANES < RET_WIDTH else 3 * RET_WIDTH + blk * LANES
            wkut_ref[blk * LANES:(blk + 1) * LANES, :] = winf_ref[:, c0:c0 + LANES].T.astype(BF16)

    wglub_ref[...] = wglu_ref[...].astype(BF16)
    woutb_ref[...] = wout_ref[...].astype(BF16)
    w1b_ref[...] = w1_ref[...].astype(BF16)
    w2b_ref[...] = w2_ref[...].astype(BF16)

    @pl.when(pl.program_id(1) == 0)
    def _():
        r_ref[...] = jnp.zeros_like(r_ref)

    tm = x_ref.shape[0]
    half = HEAD_DIM // 2
    h = _rms(x_ref[...], g_ref[...]).astype(BF16)
    pq = _dot(h, win_ref[:, :RET_WIDTH])
    pvg = _dot(h, win_ref[:, 2 * RET_WIDTH:4 * RET_WIDTH])
    pt = _dot_nt(wkut_ref[...], h)
    ncs = tm // CHUNK
    for c in range(ncs):
        ut_ref[0, pl.ds(c, SSM_WIDTH, stride=ncs), :] = pt[RET_WIDTH:, c * CHUNK:(c + 1) * CHUNK]

    row = lax.broadcasted_iota(jnp.int32, (CHUNK, CHUNK), 0)
    col = lax.broadcasted_iota(jnp.int32, (CHUNK, CHUNK), 1)
    causal = row >= col
    cosf, sinf = cosf_ref[...], sinf_ref[...]
    cost, sint = cost_ref[...], sint_ref[...]
    for hd in range(RET_HEADS):
        sl = slice(hd * HEAD_DIM, (hd + 1) * HEAD_DIM)
        qh = pq[:, sl]
        qh = qh * cosf + pltpu.roll(qh, half, 1) * sinf
        kh = pt[sl, :]
        sw = jnp.concatenate([kh[half:, :], kh[:half, :]], axis=0)
        kh = ((kh * cost + sw * sint) * ztt_ref[hd]).astype(BF16)
        g_state = r_ref[hd]
        for c in range(tm // CHUNK):
            rows = slice(c * CHUNK, (c + 1) * CHUNK)
            qc = (qh[rows, :] * xiq_ref[hd]).astype(BF16)
            kc = kh[:, rows]
            vc = pvg[rows, sl].astype(BF16)
            s = jnp.where(causal, _dot(qc, kc), 0.0).astype(BF16)
            y = _dot(jnp.concatenate([s, qc], axis=1), jnp.concatenate([vc, g_state.astype(BF16)], axis=0))
            g_state = gch_ref[hd] * (g_state + _dot(kc, vc))
            mu = jnp.mean(y, axis=-1, keepdims=True)
            yc = y - mu
            var = jnp.mean(yc * yc, axis=-1, keepdims=True)
            yn = yc * lax.rsqrt(var + NORM_EPS) * gn_ref[hd]
            gate = pvg[rows, RET_WIDTH + hd * HEAD_DIM:RET_WIDTH + (hd + 1) * HEAD_DIM]
            yret_ref[rows, sl] = (jax.nn.silu(gate) * yn).astype(BF16)
        r_ref[hd] = g_state


def _mix_in(x2, g_pre, winf, cosf, sinf, cost, sint, xiq, ztt, gch, gn, wglu, wout, w1, w2, batch, seq_len, tm):
    n = x2.shape[0]
    nseq = seq_len // tm
    nsteps = batch * nseq
    tok = lambda b, j: (b * nseq + j, 0)
    slab = lambda w: pl.BlockSpec((w.shape[0] // nsteps, w.shape[1]), tok)
    later = (wglu, wout, w1, w2)
    assert all(w.shape[0] % (BF16_SUBLANES * nsteps) == 0 for w in later)
    tile3 = lambda b, j: (b * nseq + j, 0, 0)
    const2 = lambda b, j: (0, 0)
    const3 = lambda b, j: (0, 0, 0)
    return pl.pallas_call(
        _mix_in_kernel,
        grid=(batch, nseq),
        in_specs=[
            pl.BlockSpec((tm, D_MODEL), tok),
            pl.BlockSpec((1, D_MODEL), const2),
            pl.BlockSpec((D_MODEL, 4 * RET_WIDTH + SSM_WIDTH), const2, pipeline_mode=pl.Buffered(1)),
            pl.BlockSpec((tm, HEAD_DIM), lambda b, j: (j, 0)),
            pl.BlockSpec((tm, HEAD_DIM), lambda b, j: (j, 0)),
            pl.BlockSpec((HEAD_DIM, tm), lambda b, j: (0, j)),
            pl.BlockSpec((HEAD_DIM, tm), lambda b, j: (0, j)),
            pl.BlockSpec((RET_HEADS, CHUNK, HEAD_DIM), const3),
            pl.BlockSpec((RET_HEADS, 1, tm), const3),
            pl.BlockSpec((RET_HEADS, 1, HEAD_DIM), const3),
            pl.BlockSpec((RET_HEADS, 1, HEAD_DIM), const3),
        ] + [slab(w) for w in later],
        out_specs=[
            pl.BlockSpec((tm, RET_WIDTH), tok),
            pl.BlockSpec((1, SSM_WIDTH * (tm // CHUNK), CHUNK), tile3),
        ] + [slab(w) for w in later],
        out_shape=[
            jax.ShapeDtypeStruct((n, RET_WIDTH), BF16),
            jax.ShapeDtypeStruct((n // tm, SSM_WIDTH * (tm // CHUNK), CHUNK), F32),
        ] + [jax.ShapeDtypeStruct(w.shape, BF16) for w in later],
        scratch_shapes=[pltpu.VMEM((RET_HEADS, HEAD_DIM, HEAD_DIM), F32),
                        pltpu.VMEM((D_MODEL, 4 * RET_WIDTH + SSM_WIDTH), BF16),
                        pltpu.VMEM((RET_WIDTH + SSM_WIDTH, D_MODEL), BF16)],
        compiler_params=pltpu.CompilerParams(
            dimension_semantics=("arbitrary", "arbitrary"), vmem_limit_bytes=VMEM_LIMIT_BYTES),
        name="mix_in",
    )(x2, g_pre, winf, cosf, sinf, cost, sint, xiq, ztt, gch, gn, *later)


def _s5_kernel(u_ref, pwk_ref, bc_ref, tab_ref, y_ref, sm_ref, wtap_ref, wst_ref, wct_ref, ktoe_ref, *, nc, nlev):
    m = u_ref.shape[0] * u_ref.shape[2]
    npair = SSM_GROUP // 2

    def smrow(base, i):
        return sm_ref[base + i:base + i + 1, :]

    def tabrow(base, i):
        return tab_ref[0, base + i:base + i + 1, :]

    def lo_half(shape):
        return lax.broadcasted_iota(jnp.int32, shape, 1) < HALF

    def swap(x):
        return pltpu.roll(x, HALF, 1)

    br0, bi0, cr0, ci0 = (bc_ref[0, k * SSM_GROUP:(k + 1) * SSM_GROUP, :] for k in range(4))
    sm_ref[ROW_BRR:ROW_BRR + SSM_GROUP, :] = br0 + swap(br0)
    sm_ref[ROW_BII:ROW_BII + SSM_GROUP, :] = bi0 + swap(bi0)
    sm_ref[ROW_BB1:ROW_BB1 + SSM_GROUP, :] = br0 - swap(bi0)
    sm_ref[ROW_BB2:ROW_BB2 + SSM_GROUP, :] = -bi0 - swap(br0)
    sm_ref[ROW_RC1:ROW_RC1 + SSM_GROUP, :] = cr0 + swap(cr0)
    sm_ref[ROW_RC2:ROW_RC2 + SSM_GROUP, :] = ci0 + swap(ci0)

    bb1 = sm_ref[ROW_BB1:ROW_BB1 + SSM_GROUP, :]
    bb2 = sm_ref[ROW_BB2:ROW_BB2 + SSM_GROUP, :]
    mc = lambda co: bb1 * smrow(ROW_RC1, co) + bb2 * smrow(ROW_RC2, co)
    mc_even = jnp.concatenate([mc(2 * cq) for cq in range(npair)], axis=0)
    mc_odd = jnp.concatenate([mc(2 * cq + 1) for cq in range(npair)], axis=0)
    pwk = pwk_ref[0]
    lo_sq = lo_half((2 * SSM_STATE, CHUNK))
    pw_lo = jnp.where(lo_sq, pwk, 0.0)
    pw_hi = jnp.where(lo_sq, 0.0, swap(pwk))
    wtap_ref[...] = _dot_3pass(mc_even, pw_lo) + _dot_3pass(mc_odd, pw_hi)

    sgn = jnp.where(lo_half((HALF, LANES)), -1.0, 1.0)
    pwt = pwk.T[:HALF, :]
    rev = (lax.broadcasted_iota(jnp.int32, (HALF, HALF), 0)
           + lax.broadcasted_iota(jnp.int32, (HALF, HALF), 1)) == HALF - 1
    a1 = _dot_3pass(jnp.where(rev, 1.0, 0.0), pwt)
    a2 = swap(a1) * sgn
    for ci in range(SSM_GROUP):
        wst_ref[ci * HALF:(ci + 1) * HALF, :] = (
            a1 * smrow(ROW_BRR, ci) + a2 * smrow(ROW_BII, ci)).astype(BF16)
    lam_row = pwt[1:2, :]
    lo_row = lo_half((1, LANES))
    lam_r = jnp.where(lo_row, lam_row, swap(lam_row))
    lam_i = jnp.where(lo_row, -swap(lam_row), lam_row)
    p1t = pwt * lam_r + swap(pwt) * lam_i
    b1 = p1t * (-sgn)
    b2 = -swap(p1t)
    for co in range(SSM_GROUP):
        wct_ref[co * HALF:(co + 1) * HALF, :] = (
            b1 * smrow(ROW_RC1, co) + b2 * smrow(ROW_RC2, co)).astype(BF16)

    u = [u_ref[:, ci].reshape(m, CHUNK) for ci in range(SSM_GROUP)]
    lo_m = lo_half((m, LANES))
    a_even = [jnp.where(lo_m, u[2 * cp], swap(u[2 * cp + 1])).astype(BF16) for cp in range(npair)]
    a_odd = [jnp.where(lo_m, swap(u[2 * cp]), u[2 * cp + 1]).astype(BF16) for cp in range(npair)]
    a = jnp.concatenate([jnp.concatenate(a_even, axis=1), jnp.concatenate(a_odd, axis=1)], axis=0)
    st = _dot(a, wst_ref[...])
    s_even, s_odd = st[:m, :], st[m:, :]

    def cmul(x, xs, base, i):
        lr, li = tabrow(base, i), tabrow(base, i + 1)
        return x * lr + xs * li, xs * lr - x * li

    se_s = swap(s_even)
    st, sts = cmul(s_even, se_s, TAB_L64, 0)
    st, sts = st + s_odd, sts + swap(s_odd)
    block_id = lax.broadcasted_iota(jnp.int32, (m, LANES), 0) % nc
    for lev in range(nlev):
        k = 1 << lev
        keep = block_id >= k
        inc, incs = cmul(jnp.where(keep, pltpu.roll(st, k, 0), 0.0), jnp.where(keep, pltpu.roll(sts, k, 0), 0.0),
                         TAB_LAM, 2 * lev)
        st, sts = st + inc, sts + incs
    first = block_id >= 1
    xprev_even = jnp.where(first, pltpu.roll(st, 1, 0), 0.0)
    xprev_even_s = jnp.where(first, pltpu.roll(sts, 1, 0), 0.0)
    xprev_odd = cmul(xprev_even, xprev_even_s, TAB_L64, 0)[0] + s_even
    xprev = jnp.concatenate([xprev_even, xprev_odd], axis=0).astype(BF16)

    hrow = lax.broadcasted_iota(jnp.int32, (HALF, CHUNK), 0)
    hcol = lax.broadcasted_iota(jnp.int32, (HALF, CHUNK), 1) % HALF
    valid = hcol >= hrow
    nblk = MXU_DIM // CHUNK
    for j in range(npair // nblk):
        for cc in range(nblk):
            cq = j * nblk + cc
            for ci in range(SSM_GROUP):
                tile = jnp.broadcast_to(wtap_ref[cq * SSM_GROUP + ci:cq * SSM_GROUP + ci + 1, :], (HALF, CHUNK))
                tile = pltpu.roll(tile, 0, 1, stride=1, stride_axis=0)
                ktoe_ref[j, ci * HALF:(ci + 1) * HALF, cc * CHUNK:(cc + 1) * CHUNK] = (
                    jnp.where(valid, tile, 0.0).astype(BF16))
        yj = _dot(a, ktoe_ref[j]) + _dot_nt(xprev, wct_ref[j * MXU_DIM:(j + 1) * MXU_DIM, :])
        for cc in range(nblk):
            cq = j * nblk + cc
            y_even = yj[:m, cc * CHUNK:(cc + 1) * CHUNK]
            y_odd = yj[m:, cc * CHUNK:(cc + 1) * CHUNK]
            outs = (jnp.where(lo_m, y_even, swap(y_odd)), jnp.where(lo_m, swap(y_even), y_odd))
            for i in range(2):
                co = 2 * cq + i
                yc = outs[i] + tabrow(TAB_DSK, co) * u[co]
                y_ref[:, co] = jax.nn.gelu(yc).reshape(y_ref.shape[0], y_ref.shape[2], CHUNK)


def _s5(ut4, pwk, bc, tab, nc, nlev):
    nt, _, ncs, _ = ut4.shape
    grp = lambda g: (g, 0, 0)
    ugrp = lambda g: (0, g, 0, 0)
    return pl.pallas_call(
        functools.partial(_s5_kernel, nc=nc, nlev=nlev),
        grid=(SSM_GROUPS,),
        in_specs=[
            pl.BlockSpec((nt, SSM_GROUP, ncs, CHUNK), ugrp),
            pl.BlockSpec((1, 2 * SSM_STATE, CHUNK), grp),
            pl.BlockSpec((1, 4 * SSM_GROUP, LANES), grp),
            pl.BlockSpec((1, TAB_ROWS, LANES), grp),
        ],
        out_specs=pl.BlockSpec((nt, SSM_GROUP, ncs, CHUNK), ugrp),
        out_shape=jax.ShapeDtypeStruct(ut4.shape, F32),
        scratch_shapes=[
            pltpu.VMEM((SM_ROWS, LANES), F32),
            pltpu.VMEM((SSM_GROUP * SSM_GROUP // 2, CHUNK), F32),
            pltpu.VMEM((SSM_GROUP * HALF, 2 * SSM_STATE), BF16),
            pltpu.VMEM((SSM_GROUP * HALF, 2 * SSM_STATE), BF16),
            pltpu.VMEM((SSM_GROUP * HALF // MXU_DIM, SSM_GROUP * HALF, MXU_DIM), BF16),
        ],
        compiler_params=pltpu.CompilerParams(
            dimension_semantics=("parallel",), vmem_limit_bytes=VMEM_LIMIT_BYTES),
        name="s5",
    )(ut4, pwk, bc, tab)


def _mlp_kernel(x_ref, yret_ref, yst_ref, wglu_ref, wout_ref, gpost_ref, gpre2_ref, gpost2_ref,
                w1_ref, w2_ref, o_ref, *, ff_chunk, row_splits):
    tm = x_ref.shape[0]
    ncs = tm // CHUNK
    rs, cps = tm // row_splits, ncs // row_splits
    x1s, hs = [], []
    for sp in range(row_splits):
        rows = slice(sp * rs, (sp + 1) * rs)
        yst = jnp.concatenate([yst_ref[0, pl.ds(c, SSM_WIDTH, stride=ncs), :].astype(BF16)
                               for c in range(sp * cps, (sp + 1) * cps)], axis=1)
        glu = _dot_tn(yst, wglu_ref[...])
        yssm = (glu[:, :SSM_WIDTH] * jax.nn.sigmoid(glu[:, SSM_WIDTH:])).astype(BF16)
        mix = _dot(yret_ref[rows, :], wout_ref[:RET_WIDTH, :]) + _dot(yssm, wout_ref[RET_WIDTH:, :])
        x1 = x_ref[rows, :] + _rms(mix, gpost_ref[...])
        x1s.append(x1)
        hs.append(_rms(x1, gpre2_ref[...]).astype(BF16))
    mms = [None] * row_splits
    for c in range(D_FF // ff_chunk):
        for sp in range(row_splits):
            f = jnp.maximum(_dot(hs[sp], w1_ref[:, c * ff_chunk:(c + 1) * ff_chunk]), 0.0)
            part = _dot((f * f).astype(BF16), w2_ref[c * ff_chunk:(c + 1) * ff_chunk, :])
            mms[sp] = part if mms[sp] is None else mms[sp] + part
    for sp in range(row_splits):
        o_ref[sp * rs:(sp + 1) * rs, :] = x1s[sp] + _rms(mms[sp], gpost2_ref[...])


def _mlp(x2, yret, yst, wglu, wout, gpost, gpre2, gpost2, w1, w2, tm, ff_chunk):
    n = x2.shape[0]
    const = lambda i: (0, 0)
    once = pl.Buffered(1)
    return pl.pallas_call(
        functools.partial(_mlp_kernel, ff_chunk=ff_chunk, row_splits=MLP_ROW_SPLITS),
        grid=(n // tm,),
        in_specs=[
            pl.BlockSpec((tm, D_MODEL), lambda i: (i, 0)),
            pl.BlockSpec((tm, RET_WIDTH), lambda i: (i, 0)),
            pl.BlockSpec((1, SSM_WIDTH * (tm // CHUNK), CHUNK), lambda i: (i, 0, 0)),
            pl.BlockSpec((SSM_WIDTH, 2 * SSM_WIDTH), const, pipeline_mode=once),
            pl.BlockSpec((D_MODEL, D_MODEL), const, pipeline_mode=once),
            pl.BlockSpec((1, D_MODEL), const),
            pl.BlockSpec((1, D_MODEL), const),
            pl.BlockSpec((1, D_MODEL), const),
            pl.BlockSpec((D_MODEL, D_FF), const, pipeline_mode=once),
            pl.BlockSpec((D_FF, D_MODEL), const, pipeline_mode=once),
        ],
        out_specs=pl.BlockSpec((tm, D_MODEL), lambda i: (i, 0)),
        out_shape=jax.ShapeDtypeStruct((n, D_MODEL), F32),
        compiler_params=pltpu.CompilerParams(
            dimension_semantics=("parallel",), vmem_limit_bytes=VMEM_LIMIT_BYTES),
        name="mlp",
    )(x2, yret, yst, wglu, wout, gpost, gpre2, gpost2, w1, w2)


def _rope_tables(seq_len):
    half = HEAD_DIM // 2
    inv_freq = ROPE_BASE ** (-np.arange(half, dtype=np.float64) / half)
    ang = np.arange(seq_len, dtype=np.float64)[:, None] * inv_freq[None, :]
    cos, sin = np.cos(ang), np.sin(ang)
    cosf = np.concatenate([cos, cos], axis=1)
    sinf = np.concatenate([-sin, sin], axis=1)
    kscale = HEAD_DIM ** -0.5
    f32 = lambda t: jnp.asarray(np.ascontiguousarray(t), dtype=F32)
    return f32(cosf), f32(sinf), f32(cosf.T * kscale), f32(sinf.T * kscale)


def _retention_tables(tm):
    log_gamma = np.log(1.0 - np.exp(np.linspace(math.log(1.0 / 32), math.log(1.0 / 512), RET_HEADS)))
    idx = np.arange(CHUNK, dtype=np.float64)
    xiq = np.exp((idx + 1.0 - CHUNK)[None, :] * log_gamma[:, None])
    zeta = np.exp((CHUNK - 1 - idx)[None, :] * log_gamma[:, None])
    gch = np.exp(CHUNK * log_gamma)
    xiq_b = np.broadcast_to(xiq[:, :, None], (RET_HEADS, CHUNK, HEAD_DIM))
    ztt = np.tile(zeta, (1, tm // CHUNK))[:, None, :]
    gch_b = np.broadcast_to(gch[:, None, None], (RET_HEADS, 1, HEAD_DIM))
    f32 = lambda t: jnp.asarray(np.ascontiguousarray(t), dtype=F32)
    return f32(xiq_b), f32(ztt), f32(gch_b)


def _s5_tables(lam_re, lam_im, log_dt, b_re, b_im, c_re, c_im, d_skip, nlev):
    assert nlev <= MAX_SCAN_LEVELS
    a = jnp.minimum(lam_re.astype(F32), -1e-4)
    b = lam_im.astype(F32)
    dt = jnp.exp(log_dt.astype(F32))[:, None]
    tau = jnp.arange(HALF, dtype=F32)
    mag = jnp.exp((a * dt)[:, :, None] * tau)
    ph = (b * dt)[:, :, None] * tau
    pr, pi = mag * jnp.cos(ph), mag * jnp.sin(ph)
    lr1, li = pr[:, :, 1] - 1.0, pi[:, :, 1]
    den = a * a + b * b
    c0r, c0i = (lr1 * a + li * b) / den, (li * a - lr1 * b) / den
    bre, bim = b_re.astype(F32), b_im.astype(F32)
    br = jnp.swapaxes(c0r[:, :, None] * bre - c0i[:, :, None] * bim, 1, 2)
    bi = jnp.swapaxes(c0r[:, :, None] * bim + c0i[:, :, None] * bre, 1, 2)
    cr, ci = c_re.astype(F32), c_im.astype(F32)

    pwk = jnp.concatenate([pr, pi], axis=1)
    pwk = jnp.pad(pwk, ((0, 0), (0, 0), (0, LANES - HALF)))
    cat = lambda x, y: jnp.concatenate([x, y], axis=-1)

    lam_rows = []
    for step in [float(CHUNK * (1 << lev)) for lev in range(MAX_SCAN_LEVELS)] + [float(HALF)]:
        mg = jnp.exp(a * dt * step)
        re, im = mg * jnp.cos(b * dt * step), mg * jnp.sin(b * dt * step)
        lam_rows += [cat(re, re), cat(-im, im)]
    dsk = jnp.broadcast_to(d_skip.astype(F32).reshape(SSM_GROUPS, SSM_GROUP, 1), (SSM_GROUPS, SSM_GROUP, LANES))
    tab = jnp.concatenate([jnp.stack(lam_rows, axis=1), dsk], axis=1)
    bc = jnp.concatenate([br, bi, cr, ci], axis=1)
    bc = jnp.pad(bc, ((0, 0), (0, 0), (0, LANES - SSM_STATE)))
    return pwk, bc, tab


def _tiles(seq_len):
    tm_mix = min(1024, seq_len)
    tm_mlp = tm_mix
    ff_chunk = 1024
    assert seq_len % CHUNK == 0 and seq_len % tm_mix == 0
    return tm_mix, tm_mlp, ff_chunk


def kernel(x, norm_mix_pre, norm_mix_post, w_in, ret_gn_gain, ssm_lambda_re, ssm_lambda_im, ssm_log_dt,
           ssm_b_re, ssm_b_im, ssm_c_re, ssm_c_im, ssm_d, w_glu, w_out, norm_mlp_pre, norm_mlp_post,
           w_ff1, w_ff2):
    batch, seq_len, _ = x.shape
    depth = w_in.shape[0]
    n = batch * seq_len
    nc = seq_len // CHUNK
    nlev = max(1, (nc - 1).bit_length())
    tm_mix, tm_mlp, ff_chunk = _tiles(seq_len)

    cosf, sinf, cost, sint = _rope_tables(seq_len)
    xiq, ztt, gch = _retention_tables(tm_mix)
    x2 = x.reshape(n, D_MODEL)
    for i in range(depth):
        gn = ret_gn_gain[i].astype(F32).reshape(RET_HEADS, 1, HEAD_DIM)
        yret, ut, wglu, wout, w1, w2 = _mix_in(
            x2, norm_mix_pre[i][None, :], w_in[i], cosf, sinf, cost, sint, xiq, ztt, gch, gn,
            w_glu[i], w_out[i], w_ff1[i], w_ff2[i], batch, seq_len, tm_mix)
        pwk, bc, tab = _s5_tables(
            ssm_lambda_re[i], ssm_lambda_im[i], ssm_log_dt[i], ssm_b_re[i], ssm_b_im[i],
            ssm_c_re[i], ssm_c_im[i], ssm_d[i], nlev)
        ncs = tm_mix // CHUNK
        yst = _s5(ut.reshape(n // tm_mix, SSM_WIDTH, ncs, CHUNK), pwk, bc, tab, nc, nlev)
        yst = yst.reshape(n // tm_mlp, SSM_WIDTH * ncs, CHUNK)
        x2 = _mlp(x2, yret, yst, wglu, wout,
                  norm_mix_post[i][None, :], norm_mlp_pre[i][None, :], norm_mlp_post[i][None, :],
                  w1, w2, tm_mlp, ff_chunk)
    return x2.reshape(batch, seq_len, D_MODEL)
```

```python
import functools
import math

import jax
import jax.numpy as jnp
import numpy as np
from jax import lax
from jax.experimental import pallas as pl
from jax.experimental.pallas import tpu as pltpu

D_MODEL = 1024
RET_WIDTH = 512
RET_HEADS = 4
HEAD_DIM = 128
CHUNK = 128
HALF = CHUNK // 2
ROPE_BASE = 10000.0
SSM_WIDTH = 512
SSM_GROUP = 16
SSM_GROUPS = 32
SSM_STATE = 64
D_FF = 4096
NORM_EPS = 1e-6
LANES = 128
MXU_DIM = 256
BF16_SUBLANES = 16
MLP_ROW_SPLITS = 2
S5_GROUPS_PER_STEP = 2

BF16 = jnp.bfloat16
F32 = jnp.float32

VMEM_LIMIT_BYTES = 56 * 1024 * 1024

ROW_BRR, ROW_BII, ROW_BB1, ROW_BB2, ROW_RC1, ROW_RC2, SM_ROWS = 0, 16, 32, 48, 64, 80, 96
TAB_LAM, TAB_L64, TAB_DSK, TAB_ROWS = 0, 14, 16, 32
MAX_SCAN_LEVELS = (TAB_L64 - TAB_LAM) // 2


def _dot(a, b):
    return jnp.dot(a, b, preferred_element_type=F32)


def _dot_nt(a, b):
    return lax.dot_general(a, b, (((1,), (1,)), ((), ())), preferred_element_type=F32)


def _dot_tn(a, b):
    return lax.dot_general(a, b, (((0,), (0,)), ((), ())), preferred_element_type=F32)


def _dot_3pass(a, b):
    a_hi = a.astype(BF16)
    a_lo = (a - a_hi.astype(F32)).astype(BF16)
    b_hi = b.astype(BF16)
    b_lo = (b - b_hi.astype(F32)).astype(BF16)
    return _dot(a_hi, b_hi) + (_dot(a_hi, b_lo) + _dot(a_lo, b_hi))


def _rms(x, g):
    ms = jnp.mean(x * x, axis=-1, keepdims=True)
    return x * lax.rsqrt(ms + NORM_EPS) * g


def _mix_in_kernel(x_ref, g_ref, winf_ref, cosf_ref, sinf_ref, cost_ref, sint_ref,
                   xiq_ref, ztt_ref, gch_ref, gn_ref, wglu_ref, wout_ref, w1_ref, w2_ref,
                   yret_ref, ut_ref, wglub_ref, woutb_ref, w1b_ref, w2b_ref, r_ref, win_ref, wkut_ref):
    @pl.when((pl.program_id(0) == 0) & (pl.program_id(1) == 0))
    def _():
        for c0 in range(0, win_ref.shape[1], RET_WIDTH):
            win_ref[:, c0:c0 + RET_WIDTH] = winf_ref[:, c0:c0 + RET_WIDTH].astype(BF16)
        for blk in range((RET_WIDTH + SSM_WIDTH) // LANES):
            c0 = RET_WIDTH + blk * LANES if blk * LANES < RET_WIDTH else 3 * RET_WIDTH + blk * LANES
            wkut_ref[blk * LANES:(blk + 1) * LANES, :] = winf_ref[:, c0:c0 + LANES].T.astype(BF16)

    wglub_ref[...] = wglu_ref[...].astype(BF16)
    woutb_ref[...] = wout_ref[...].astype(BF16)
    w1b_ref[...] = w1_ref[...].astype(BF16)
    w2b_ref[...] = w2_ref[...].astype(BF16)

    @pl.when(pl.program_id(1) == 0)
    def _():
        r_ref[...] = jnp.zeros_like(r_ref)

    tm = x_ref.shape[0]
    half = HEAD_DIM // 2
    h = _rms(x_ref[...], g_ref[...]).astype(BF16)
    pq = _dot(h, win_ref[:, :RET_WIDTH])
    pvg = _dot(h, win_ref[:, 2 * RET_WIDTH:4 * RET_WIDTH])
    pt = _dot_nt(wkut_ref[...], h)
    ncs = tm // CHUNK
    for c in range(ncs):
        ut_ref[0, pl.ds(c, SSM_WIDTH, stride=ncs), :] = pt[RET_WIDTH:, c * CHUNK:(c + 1) * CHUNK]

    row = lax.broadcasted_iota(jnp.int32, (CHUNK, CHUNK), 0)
    col = lax.broadcasted_iota(jnp.int32, (CHUNK, CHUNK), 1)
    causal = row >= col
    cosf, sinf = cosf_ref[...], sinf_ref[...]
    cost, sint = cost_ref[...], sint_ref[...]
    for hd in range(RET_HEADS):
        sl = slice(hd * HEAD_DIM, (hd + 1) * HEAD_DIM)
        qh = pq[:, sl]
        qh = qh * cosf + pltpu.roll(qh, half, 1) * sinf
        kh = pt[sl, :]
        sw = jnp.concatenate([kh[half:, :], kh[:half, :]], axis=0)
        kh = ((kh * cost + sw * sint) * ztt_ref[hd]).astype(BF16)
        g_state = r_ref[hd]
        for c in range(tm // CHUNK):
            rows = slice(c * CHUNK, (c + 1) * CHUNK)
            qc = (qh[rows, :] * xiq_ref[hd]).astype(BF16)
            kc = kh[:, rows]
            vc = pvg[rows, sl].astype(BF16)
            s = jnp.where(causal, _dot(qc, kc), 0.0).astype(BF16)
            y = _dot(jnp.concatenate([s, qc], axis=1), jnp.concatenate([vc, g_state.astype(BF16)], axis=0))
            g_state = gch_ref[hd] * (g_state + _dot(kc, vc))
            mu = jnp.mean(y, axis=-1, keepdims=True)
            yc = y - mu
            var = jnp.mean(yc * yc, axis=-1, keepdims=True)
            yn = yc * lax.rsqrt(var + NORM_EPS) * gn_ref[hd]
            gate = pvg[rows, RET_WIDTH + hd * HEAD_DIM:RET_WIDTH + (hd + 1) * HEAD_DIM]
            yret_ref[rows, sl] = (jax.nn.silu(gate) * yn).astype(BF16)
        r_ref[hd] = g_state


def _mix_in(x2, g_pre, winf, cosf, sinf, cost, sint, xiq, ztt, gch, gn, wglu, wout, w1, w2, batch, seq_len, tm):
    n = x2.shape[0]
    nseq = seq_len // tm
    nsteps = batch * nseq
    tok = lambda b, j: (b * nseq + j, 0)
    slab = lambda w: pl.BlockSpec((w.shape[0] // nsteps, w.shape[1]), tok)
    later = (wglu, wout, w1, w2)
    assert all(w.shape[0] % (BF16_SUBLANES * nsteps) == 0 for w in later)
    tile3 = lambda b, j: (b * nseq + j, 0, 0)
    const2 = lambda b, j: (0, 0)
    const3 = lambda b, j: (0, 0, 0)
    return pl.pallas_call(
        _mix_in_kernel,
        grid=(batch, nseq),
        in_specs=[
            pl.BlockSpec((tm, D_MODEL), tok),
            pl.BlockSpec((1, D_MODEL), const2),
            pl.BlockSpec((D_MODEL, 4 * RET_WIDTH + SSM_WIDTH), const2, pipeline_mode=pl.Buffered(1)),
            pl.BlockSpec((tm, HEAD_DIM), lambda b, j: (j, 0)),
            pl.BlockSpec((tm, HEAD_DIM), lambda b, j: (j, 0)),
            pl.BlockSpec((HEAD_DIM, tm), lambda b, j: (0, j)),
            pl.BlockSpec((HEAD_DIM, tm), lambda b, j: (0, j)),
            pl.BlockSpec((RET_HEADS, CHUNK, HEAD_DIM), const3),
            pl.BlockSpec((RET_HEADS, 1, tm), const3),
            pl.BlockSpec((RET_HEADS, 1, HEAD_DIM), const3),
            pl.BlockSpec((RET_HEADS, 1, HEAD_DIM), const3),
        ] + [slab(w) for w in later],
        out_specs=[
            pl.BlockSpec((tm, RET_WIDTH), tok),
            pl.BlockSpec((1, SSM_WIDTH * (tm // CHUNK), CHUNK), tile3),
        ] + [slab(w) for w in later],
        out_shape=[
            jax.ShapeDtypeStruct((n, RET_WIDTH), BF16),
            jax.ShapeDtypeStruct((n // tm, SSM_WIDTH * (tm // CHUNK), CHUNK), F32),
        ] + [jax.ShapeDtypeStruct(w.shape, BF16) for w in later],
        scratch_shapes=[pltpu.VMEM((RET_HEADS, HEAD_DIM, HEAD_DIM), F32),
                        pltpu.VMEM((D_MODEL, 4 * RET_WIDTH + SSM_WIDTH), BF16),
                        pltpu.VMEM((RET_WIDTH + SSM_WIDTH, D_MODEL), BF16)],
        compiler_params=pltpu.CompilerParams(
            dimension_semantics=("arbitrary", "arbitrary"), vmem_limit_bytes=VMEM_LIMIT_BYTES),
        name="mix_in",
    )(x2, g_pre, winf, cosf, sinf, cost, sint, xiq, ztt, gch, gn, *later)


def _s5_kernel(u_ref, pwk_ref, bc_ref, tab_ref, y_ref, sm_ref, wtap_ref, wst_ref, wct_ref, ktoe_ref, *, nc, nlev):
    for gi in range(pwk_ref.shape[0]):
        ch = slice(gi * SSM_GROUP, (gi + 1) * SSM_GROUP)
        _s5_group(u_ref.at[:, ch], pwk_ref.at[gi], bc_ref.at[gi], tab_ref.at[gi], y_ref.at[:, ch],
                  sm_ref.at[gi], wtap_ref.at[gi], wst_ref.at[gi], wct_ref.at[gi], ktoe_ref.at[gi], nc, nlev)


def _s5_group(u_ref, pwk_ref, bc_ref, tab_ref, y_ref, sm_ref, wtap_ref, wst_ref, wct_ref, ktoe_ref, nc, nlev):
    m = u_ref.shape[0] * u_ref.shape[2]
    npair = SSM_GROUP // 2

    def smrow(base, i):
        return sm_ref[base + i:base + i + 1, :]

    def tabrow(base, i):
        return tab_ref[base + i:base + i + 1, :]

    def lo_half(shape):
        return lax.broadcasted_iota(jnp.int32, shape, 1) < HALF

    def swap(x):
        return pltpu.roll(x, HALF, 1)

    br0, bi0, cr0, ci0 = (bc_ref[k * SSM_GROUP:(k + 1) * SSM_GROUP, :] for k in range(4))
    sm_ref[ROW_BRR:ROW_BRR + SSM_GROUP, :] = br0 + swap(br0)
    sm_ref[ROW_BII:ROW_BII + SSM_GROUP, :] = bi0 + swap(bi0)
    sm_ref[ROW_BB1:ROW_BB1 + SSM_GROUP, :] = br0 - swap(bi0)
    sm_ref[ROW_BB2:ROW_BB2 + SSM_GROUP, :] = -bi0 - swap(br0)
    sm_ref[ROW_RC1:ROW_RC1 + SSM_GROUP, :] = cr0 + swap(cr0)
    sm_ref[ROW_RC2:ROW_RC2 + SSM_GROUP, :] = ci0 + swap(ci0)

    bb1 = sm_ref[ROW_BB1:ROW_BB1 + SSM_GROUP, :]
    bb2 = sm_ref[ROW_BB2:ROW_BB2 + SSM_GROUP, :]
    mc = lambda co: bb1 * smrow(ROW_RC1, co) + bb2 * smrow(ROW_RC2, co)
    mc_even = jnp.concatenate([mc(2 * cq) for cq in range(npair)], axis=0)
    mc_odd = jnp.concatenate([mc(2 * cq + 1) for cq in range(npair)], axis=0)
    pwk = pwk_ref[...]
    lo_sq = lo_half((2 * SSM_STATE, CHUNK))
    pw_lo = jnp.where(lo_sq, pwk, 0.0)
    pw_hi = jnp.where(lo_sq, 0.0, swap(pwk))
    wtap_ref[...] = _dot_3pass(mc_even, pw_lo) + _dot_3pass(mc_odd, pw_hi)

    sgn = jnp.where(lo_half((HALF, LANES)), -1.0, 1.0)
    pwt = pwk.T[:HALF, :]
    rev = (lax.broadcasted_iota(jnp.int32, (HALF, HALF), 0)
           + lax.broadcasted_iota(jnp.int32, (HALF, HALF), 1)) == HALF - 1
    a1 = _dot_3pass(jnp.where(rev, 1.0, 0.0), pwt)
    a2 = swap(a1) * sgn
    for ci in range(SSM_GROUP):
        wst_ref[ci * HALF:(ci + 1) * HALF, :] = (
            a1 * smrow(ROW_BRR, ci) + a2 * smrow(ROW_BII, ci)).astype(BF16)
    lam_row = pwt[1:2, :]
    lo_row = lo_half((1, LANES))
    lam_r = jnp.where(lo_row, lam_row, swap(lam_row))
    lam_i = jnp.where(lo_row, -swap(lam_row), lam_row)
    p1t = pwt * lam_r + swap(pwt) * lam_i
    b1 = p1t * (-sgn)
    b2 = -swap(p1t)
    for co in range(SSM_GROUP):
        wct_ref[co * HALF:(co + 1) * HALF, :] = (
            b1 * smrow(ROW_RC1, co) + b2 * smrow(ROW_RC2, co)).astype(BF16)

    u = [u_ref[:, ci].reshape(m, CHUNK) for ci in range(SSM_GROUP)]
    lo_m = lo_half((m, LANES))
    a_even = [jnp.where(lo_m, u[2 * cp], swap(u[2 * cp + 1])).astype(BF16) for cp in range(npair)]
    a_odd = [jnp.where(lo_m, swap(u[2 * cp]), u[2 * cp + 1]).astype(BF16) for cp in range(npair)]
    a = jnp.concatenate([jnp.concatenate(a_even, axis=1), jnp.concatenate(a_odd, axis=1)], axis=0)
    st = _dot(a, wst_ref[...])
    s_even, s_odd = st[:m, :], st[m:, :]

    def cmul(x, xs, base, i):
        lr, li = tabrow(base, i), tabrow(base, i + 1)
        return x * lr + xs * li, xs * lr - x * li

    se_s = swap(s_even)
    st, sts = cmul(s_even, se_s, TAB_L64, 0)
    st, sts = st + s_odd, sts + swap(s_odd)
    block_id = lax.broadcasted_iota(jnp.int32, (m, LANES), 0) % nc
    for lev in range(nlev):
        k = 1 << lev
        keep = block_id >= k
        inc, incs = cmul(jnp.where(keep, pltpu.roll(st, k, 0), 0.0), jnp.where(keep, pltpu.roll(sts, k, 0), 0.0),
                         TAB_LAM, 2 * lev)
        st, sts = st + inc, sts + incs
    first = block_id >= 1
    xprev_even = jnp.where(first, pltpu.roll(st, 1, 0), 0.0)
    xprev_even_s = jnp.where(first, pltpu.roll(sts, 1, 0), 0.0)
    xprev_odd = cmul(xprev_even, xprev_even_s, TAB_L64, 0)[0] + s_even
    xprev = jnp.concatenate([xprev_even, xprev_odd], axis=0).astype(BF16)

    hrow = lax.broadcasted_iota(jnp.int32, (HALF, CHUNK), 0)
    hcol = lax.broadcasted_iota(jnp.int32, (HALF, CHUNK), 1) % HALF
    valid = hcol >= hrow
    nblk = MXU_DIM // CHUNK
    for j in range(npair // nblk):
        for cc in range(nblk):
            cq = j * nblk + cc
            for ci in range(SSM_GROUP):
                tile = jnp.broadcast_to(wtap_ref[cq * SSM_GROUP + ci:cq * SSM_GROUP + ci + 1, :], (HALF, CHUNK))
                tile = pltpu.roll(tile, 0, 1, stride=1, stride_axis=0)
                ktoe_ref[j, ci * HALF:(ci + 1) * HALF, cc * CHUNK:(cc + 1) * CHUNK] = (
                    jnp.where(valid, tile, 0.0).astype(BF16))
        yj = _dot(a, ktoe_ref[j]) + _dot_nt(xprev, wct_ref[j * MXU_DIM:(j + 1) * MXU_DIM, :])
        for cc in range(nblk):
            cq = j * nblk + cc
            y_even = yj[:m, cc * CHUNK:(cc + 1) * CHUNK]
            y_odd = yj[m:, cc * CHUNK:(cc + 1) * CHUNK]
            outs = (jnp.where(lo_m, y_even, swap(y_odd)), jnp.where(lo_m, swap(y_even), y_odd))
            for i in range(2):
                co = 2 * cq + i
                yc = outs[i] + tabrow(TAB_DSK, co) * u[co]
                y_ref[:, co] = jax.nn.gelu(yc).reshape(y_ref.shape[0], y_ref.shape[2], CHUNK)


def _s5(ut4, pwk, bc, tab, nc, nlev):
    nt, _, ncs, _ = ut4.shape
    gps = S5_GROUPS_PER_STEP
    grp = lambda g: (g, 0, 0)
    ugrp = lambda g: (0, g, 0, 0)
    return pl.pallas_call(
        functools.partial(_s5_kernel, nc=nc, nlev=nlev),
        grid=(SSM_GROUPS // gps,),
        in_specs=[
            pl.BlockSpec((nt, gps * SSM_GROUP, ncs, CHUNK), ugrp),
            pl.BlockSpec((gps, 2 * SSM_STATE, CHUNK), grp),
            pl.BlockSpec((gps, 4 * SSM_GROUP, LANES), grp),
            pl.BlockSpec((gps, TAB_ROWS, LANES), grp),
        ],
        out_specs=pl.BlockSpec((nt, gps * SSM_GROUP, ncs, CHUNK), ugrp),
        out_shape=jax.ShapeDtypeStruct(ut4.shape, F32),
        scratch_shapes=[
            pltpu.VMEM((gps, SM_ROWS, LANES), F32),
            pltpu.VMEM((gps, SSM_GROUP * SSM_GROUP // 2, CHUNK), F32),
            pltpu.VMEM((gps, SSM_GROUP * HALF, 2 * SSM_STATE), BF16),
            pltpu.VMEM((gps, SSM_GROUP * HALF, 2 * SSM_STATE), BF16),
            pltpu.VMEM((gps, SSM_GROUP * HALF // MXU_DIM, SSM_GROUP * HALF, MXU_DIM), BF16),
        ],
        compiler_params=pltpu.CompilerParams(
            dimension_semantics=("parallel",), vmem_limit_bytes=VMEM_LIMIT_BYTES),
        name="s5",
    )(ut4, pwk, bc, tab)


def _mlp_kernel(x_ref, yret_ref, yst_ref, wglu_ref, wout_ref, gpost_ref, gpre2_ref, gpost2_ref,
                w1_ref, w2_ref, o_ref, *, ff_chunk, row_splits):
    tm = x_ref.shape[0]
    ncs = tm // CHUNK
    rs, cps = tm // row_splits, ncs // row_splits
    x1s, hs = [], []
    for sp in range(row_splits):
        rows = slice(sp * rs, (sp + 1) * rs)
        yst = jnp.concatenate([yst_ref[0, pl.ds(c, SSM_WIDTH, stride=ncs), :].astype(BF16)
                               for c in range(sp * cps, (sp + 1) * cps)], axis=1)
        glu = _dot_tn(yst, wglu_ref[...])
        yssm = (glu[:, :SSM_WIDTH] * jax.nn.sigmoid(glu[:, SSM_WIDTH:])).astype(BF16)
        mix = _dot(yret_ref[rows, :], wout_ref[:RET_WIDTH, :]) + _dot(yssm, wout_ref[RET_WIDTH:, :])
        x1 = x_ref[rows, :] + _rms(mix, gpost_ref[...])
        x1s.append(x1)
        hs.append(_rms(x1, gpre2_ref[...]).astype(BF16))
    mms = [None] * row_splits
    for c in range(D_FF // ff_chunk):
        for sp in range(row_splits):
            f = jnp.maximum(_dot(hs[sp], w1_ref[:, c * ff_chunk:(c + 1) * ff_chunk]), 0.0)
            part = _dot((f * f).astype(BF16), w2_ref[c * ff_chunk:(c + 1) * ff_chunk, :])
            mms[sp] = part if mms[sp] is None else mms[sp] + part
    for sp in range(row_splits):
        o_ref[sp * rs:(sp + 1) * rs, :] = x1s[sp] + _rms(mms[sp], gpost2_ref[...])


def _mlp(x2, yret, yst, wglu, wout, gpost, gpre2, gpost2, w1, w2, tm, ff_chunk):
    n = x2.shape[0]
    const = lambda i: (0, 0)
    once = pl.Buffered(1)
    return pl.pallas_call(
        functools.partial(_mlp_kernel, ff_chunk=ff_chunk, row_splits=MLP_ROW_SPLITS),
        grid=(n // tm,),
        in_specs=[
            pl.BlockSpec((tm, D_MODEL), lambda i: (i, 0)),
            pl.BlockSpec((tm, RET_WIDTH), lambda i: (i, 0)),
            pl.BlockSpec((1, SSM_WIDTH * (tm // CHUNK), CHUNK), lambda i: (i, 0, 0)),
            pl.BlockSpec((SSM_WIDTH, 2 * SSM_WIDTH), const, pipeline_mode=once),
            pl.BlockSpec((D_MODEL, D_MODEL), const, pipeline_mode=once),
            pl.BlockSpec((1, D_MODEL), const),
            pl.BlockSpec((1, D_MODEL), const),
            pl.BlockSpec((1, D_MODEL), const),
            pl.BlockSpec((D_MODEL, D_FF), const, pipeline_mode=once),
            pl.BlockSpec((D_FF, D_MODEL), const, pipeline_mode=once),
        ],
        out_specs=pl.BlockSpec((tm, D_MODEL), lambda i: (i, 0)),
        out_shape=jax.ShapeDtypeStruct((n, D_MODEL), F32),
        compiler_params=pltpu.CompilerParams(
            dimension_semantics=("parallel",), vmem_limit_bytes=VMEM_LIMIT_BYTES),
        name="mlp",
    )(x2, yret, yst, wglu, wout, gpost, gpre2, gpost2, w1, w2)


def _rope_tables(seq_len):
    half = HEAD_DIM // 2
    inv_freq = ROPE_BASE ** (-np.arange(half, dtype=np.float64) / half)
    ang = np.arange(seq_len, dtype=np.float64)[:, None] * inv_freq[None, :]
    cos, sin = np.cos(ang), np.sin(ang)
    cosf = np.concatenate([cos, cos], axis=1)
    sinf = np.concatenate([-sin, sin], axis=1)
    kscale = HEAD_DIM ** -0.5
    f32 = lambda t: jnp.asarray(np.ascontiguousarray(t), dtype=F32)
    return f32(cosf), f32(sinf), f32(cosf.T * kscale), f32(sinf.T * kscale)


def _retention_tables(tm):
    log_gamma = np.log(1.0 - np.exp(np.linspace(math.log(1.0 / 32), math.log(1.0 / 512), RET_HEADS)))
    idx = np.arange(CHUNK, dtype=np.float64)
    xiq = np.exp((idx + 1.0 - CHUNK)[None, :] * log_gamma[:, None])
    zeta = np.exp((CHUNK - 1 - idx)[None, :] * log_gamma[:, None])
    gch = np.exp(CHUNK * log_gamma)
    xiq_b = np.broadcast_to(xiq[:, :, None], (RET_HEADS, CHUNK, HEAD_DIM))
    ztt = np.tile(zeta, (1, tm // CHUNK))[:, None, :]
    gch_b = np.broadcast_to(gch[:, None, None], (RET_HEADS, 1, HEAD_DIM))
    f32 = lambda t: jnp.asarray(np.ascontiguousarray(t), dtype=F32)
    return f32(xiq_b), f32(ztt), f32(gch_b)


def _s5_tables(lam_re, lam_im, log_dt, b_re, b_im, c_re, c_im, d_skip, nlev):
    assert nlev <= MAX_SCAN_LEVELS
    a = jnp.minimum(lam_re.astype(F32), -1e-4)
    b = lam_im.astype(F32)
    dt = jnp.exp(log_dt.astype(F32))[:, None]
    tau = jnp.arange(HALF, dtype=F32)
    mag = jnp.exp((a * dt)[:, :, None] * tau)
    ph = (b * dt)[:, :, None] * tau
    pr, pi = mag * jnp.cos(ph), mag * jnp.sin(ph)
    lr1, li = pr[:, :, 1] - 1.0, pi[:, :, 1]
    den = a * a + b * b
    c0r, c0i = (lr1 * a + li * b) / den, (li * a - lr1 * b) / den
    bre, bim = b_re.astype(F32), b_im.astype(F32)
    br = jnp.swapaxes(c0r[:, :, None] * bre - c0i[:, :, None] * bim, 1, 2)
    bi = jnp.swapaxes(c0r[:, :, None] * bim + c0i[:, :, None] * bre, 1, 2)
    cr, ci = c_re.astype(F32), c_im.astype(F32)

    pwk = jnp.concatenate([pr, pi], axis=1)
    pwk = jnp.pad(pwk, ((0, 0), (0, 0), (0, LANES - HALF)))
    cat = lambda x, y: jnp.concatenate([x, y], axis=-1)

    lam_rows = []
    for step in [float(CHUNK * (1 << lev)) for lev in range(MAX_SCAN_LEVELS)] + [float(HALF)]:
        mg = jnp.exp(a * dt * step)
        re, im = mg * jnp.cos(b * dt * step), mg * jnp.sin(b * dt * step)
        lam_rows += [cat(re, re), cat(-im, im)]
    dsk = jnp.broadcast_to(d_skip.astype(F32).reshape(SSM_GROUPS, SSM_GROUP, 1), (SSM_GROUPS, SSM_GROUP, LANES))
    tab = jnp.concatenate([jnp.stack(lam_rows, axis=1), dsk], axis=1)
    bc = jnp.concatenate([br, bi, cr, ci], axis=1)
    bc = jnp.pad(bc, ((0, 0), (0, 0), (0, LANES - SSM_STATE)))
    return pwk, bc, tab


def _tiles(seq_len):
    tm_mix = min(1024, seq_len)
    tm_mlp = tm_mix
    ff_chunk = 1024
    assert seq_len % CHUNK == 0 and seq_len % tm_mix == 0
    return tm_mix, tm_mlp, ff_chunk


def kernel(x, norm_mix_pre, norm_mix_post, w_in, ret_gn_gain, ssm_lambda_re, ssm_lambda_im, ssm_log_dt,
           ssm_b_re, ssm_b_im, ssm_c_re, ssm_c_im, ssm_d, w_glu, w_out, norm_mlp_pre, norm_mlp_post,
           w_ff1, w_ff2):
    batch, seq_len, _ = x.shape
    depth = w_in.shape[0]
    n = batch * seq_len
    nc = seq_len // CHUNK
    nlev = max(1, (nc - 1).bit_length())
    tm_mix, tm_mlp, ff_chunk = _tiles(seq_len)

    cosf, sinf, cost, sint = _rope_tables(seq_len)
    xiq, ztt, gch = _retention_tables(tm_mix)
    x2 = x.reshape(n, D_MODEL)
    for i in range(depth):
        gn = ret_gn_gain[i].astype(F32).reshape(RET_HEADS, 1, HEAD_DIM)
        yret, ut, wglu, wout, w1, w2 = _mix_in(
            x2, norm_mix_pre[i][None, :], w_in[i], cosf, sinf, cost, sint, xiq, ztt, gch, gn,
            w_glu[i], w_out[i], w_ff1[i], w_ff2[i], batch, seq_len, tm_mix)
        pwk, bc, tab = _s5_tables(
            ssm_lambda_re[i], ssm_lambda_im[i], ssm_log_dt[i], ssm_b_re[i], ssm_b_im[i],
            ssm_c_re[i], ssm_c_im[i], ssm_d[i], nlev)
        ncs = tm_mix // CHUNK
        yst = _s5(ut.reshape(n // tm_mix, SSM_WIDTH, ncs, CHUNK), pwk, bc, tab, nc, nlev)
        yst = yst.reshape(n // tm_mlp, SSM_WIDTH * ncs, CHUNK)
        x2 = _mlp(x2, yret, yst, wglu, wout,
                  norm_mix_post[i][None, :], norm_mlp_pre[i][None, :], norm_mlp_post[i][None, :],
                  w1, w2, tm_mlp, ff_chunk)
    return x2.reshape(batch, seq_len, D_MODEL)
```

```python
import functools
import math

import jax
import jax.numpy as jnp
import numpy as np
from jax import lax
from jax.experimental import pallas as pl
from jax.experimental.pallas import tpu as pltpu

D_MODEL = 1024
RET_WIDTH = 512
RET_HEADS = 4
HEAD_DIM = 128
CHUNK = 128
HALF = CHUNK // 2
ROPE_BASE = 10000.0
SSM_WIDTH = 512
SSM_GROUP = 16
SSM_GROUPS = 32
SSM_STATE = 64
D_FF = 4096
NORM_EPS = 1e-6
LANES = 128
MXU_DIM = 256
BF16_SUBLANES = 16
MLP_ROW_SPLITS = 2
S5_ROW_PARTS = 4

BF16 = jnp.bfloat16
F32 = jnp.float32

VMEM_LIMIT_BYTES = 56 * 1024 * 1024

ROW_BRR, ROW_BII, ROW_BB1, ROW_BB2, ROW_RC1, ROW_RC2, SM_ROWS = 0, 16, 32, 48, 64, 80, 96
TAB_LAM, TAB_L64, TAB_DSK, TAB_ROWS = 0, 14, 16, 32
MAX_SCAN_LEVELS = (TAB_L64 - TAB_LAM) // 2


def _dot(a, b):
    return jnp.dot(a, b, preferred_element_type=F32)


def _dot_nt(a, b):
    return lax.dot_general(a, b, (((1,), (1,)), ((), ())), preferred_element_type=F32)


def _dot_tn(a, b):
    return lax.dot_general(a, b, (((0,), (0,)), ((), ())), preferred_element_type=F32)


def _dot_3pass(a, b):
    a_hi = a.astype(BF16)
    a_lo = (a - a_hi.astype(F32)).astype(BF16)
    b_hi = b.astype(BF16)
    b_lo = (b - b_hi.astype(F32)).astype(BF16)
    return _dot(a_hi, b_hi) + (_dot(a_hi, b_lo) + _dot(a_lo, b_hi))


def _rms(x, g):
    ms = jnp.mean(x * x, axis=-1, keepdims=True)
    return x * lax.rsqrt(ms + NORM_EPS) * g


def _mix_in_kernel(x_ref, g_ref, winf_ref, cosf_ref, sinf_ref, cost_ref, sint_ref,
                   xiq_ref, ztt_ref, gch_ref, gn_ref, wglu_ref, wout_ref, w1_ref, w2_ref,
                   yret_ref, ut_ref, wglub_ref, woutb_ref, w1b_ref, w2b_ref, r_ref, win_ref, wkut_ref):
    @pl.when((pl.program_id(0) == 0) & (pl.program_id(1) == 0))
    def _():
        for c0 in range(0, win_ref.shape[1], RET_WIDTH):
            win_ref[:, c0:c0 + RET_WIDTH] = winf_ref[:, c0:c0 + RET_WIDTH].astype(BF16)
        for blk in range((RET_WIDTH + SSM_WIDTH) // LANES):
            c0 = RET_WIDTH + blk * LANES if blk * LANES < RET_WIDTH else 3 * RET_WIDTH + blk * LANES
            wkut_ref[blk * LANES:(blk + 1) * LANES, :] = winf_ref[:, c0:c0 + LANES].T.astype(BF16)

    wglub_ref[...] = wglu_ref[...].astype(BF16)
    woutb_ref[...] = wout_ref[...].astype(BF16)
    w1b_ref[...] = w1_ref[...].astype(BF16)
    w2b_ref[...] = w2_ref[...].astype(BF16)

    @pl.when(pl.program_id(1) == 0)
    def _():
        r_ref[...] = jnp.zeros_like(r_ref)

    tm = x_ref.shape[0]
    half = HEAD_DIM // 2
    h = _rms(x_ref[...], g_ref[...]).astype(BF16)
    pq = _dot(h, win_ref[:, :RET_WIDTH])
    pvg = _dot(h, win_ref[:, 2 * RET_WIDTH:4 * RET_WIDTH])
    pt = _dot_nt(wkut_ref[...], h)
    ncs = tm // CHUNK
    for c in range(ncs):
        ut_ref[0, pl.ds(c, SSM_WIDTH, stride=ncs), :] = pt[RET_WIDTH:, c * CHUNK:(c + 1) * CHUNK]

    row = lax.broadcasted_iota(jnp.int32, (CHUNK, CHUNK), 0)
    col = lax.broadcasted_iota(jnp.int32, (CHUNK, CHUNK), 1)
    causal = row >= col
    cosf, sinf = cosf_ref[...], sinf_ref[...]
    cost, sint = cost_ref[...], sint_ref[...]
    for hd in range(RET_HEADS):
        sl = slice(hd * HEAD_DIM, (hd + 1) * HEAD_DIM)
        qh = pq[:, sl]
        qh = qh * cosf + pltpu.roll(qh, half, 1) * sinf
        kh = pt[sl, :]
        sw = jnp.concatenate([kh[half:, :], kh[:half, :]], axis=0)
        kh = ((kh * cost + sw * sint) * ztt_ref[hd]).astype(BF16)
        g_state = r_ref[hd]
        for c in range(tm // CHUNK):
            rows = slice(c * CHUNK, (c + 1) * CHUNK)
            qc = (qh[rows, :] * xiq_ref[hd]).astype(BF16)
            kc = kh[:, rows]
            vc = pvg[rows, sl].astype(BF16)
            s = jnp.where(causal, _dot(qc, kc), 0.0).astype(BF16)
            y = _dot(jnp.concatenate([s, qc], axis=1), jnp.concatenate([vc, g_state.astype(BF16)], axis=0))
            g_state = gch_ref[hd] * (g_state + _dot(kc, vc))
            mu = jnp.mean(y, axis=-1, keepdims=True)
            yc = y - mu
            var = jnp.mean(yc * yc, axis=-1, keepdims=True)
            yn = yc * lax.rsqrt(var + NORM_EPS) * gn_ref[hd]
            gate = pvg[rows, RET_WIDTH + hd * HEAD_DIM:RET_WIDTH + (hd + 1) * HEAD_DIM]
            yret_ref[rows, sl] = (jax.nn.silu(gate) * yn).astype(BF16)
        r_ref[hd] = g_state


def _mix_in(x2, g_pre, winf, cosf, sinf, cost, sint, xiq, ztt, gch, gn, wglu, wout, w1, w2, batch, seq_len, tm):
    n = x2.shape[0]
    nseq = seq_len // tm
    nsteps = batch * nseq
    tok = lambda b, j: (b * nseq + j, 0)
    slab = lambda w: pl.BlockSpec((w.shape[0] // nsteps, w.shape[1]), tok)
    later = (wglu, wout, w1, w2)
    assert all(w.shape[0] % (BF16_SUBLANES * nsteps) == 0 for w in later)
    tile3 = lambda b, j: (b * nseq + j, 0, 0)
    const2 = lambda b, j: (0, 0)
    const3 = lambda b, j: (0, 0, 0)
    return pl.pallas_call(
        _mix_in_kernel,
        grid=(batch, nseq),
        in_specs=[
            pl.BlockSpec((tm, D_MODEL), tok),
            pl.BlockSpec((1, D_MODEL), const2),
            pl.BlockSpec((D_MODEL, 4 * RET_WIDTH + SSM_WIDTH), const2, pipeline_mode=pl.Buffered(1)),
            pl.BlockSpec((tm, HEAD_DIM), lambda b, j: (j, 0)),
            pl.BlockSpec((tm, HEAD_DIM), lambda b, j: (j, 0)),
            pl.BlockSpec((HEAD_DIM, tm), lambda b, j: (0, j)),
            pl.BlockSpec((HEAD_DIM, tm), lambda b, j: (0, j)),
            pl.BlockSpec((RET_HEADS, CHUNK, HEAD_DIM), const3),
            pl.BlockSpec((RET_HEADS, 1, tm), const3),
            pl.BlockSpec((RET_HEADS, 1, HEAD_DIM), const3),
            pl.BlockSpec((RET_HEADS, 1, HEAD_DIM), const3),
        ] + [slab(w) for w in later],
        out_specs=[
            pl.BlockSpec((tm, RET_WIDTH), tok),
            pl.BlockSpec((1, SSM_WIDTH * (tm // CHUNK), CHUNK), tile3),
        ] + [slab(w) for w in later],
        out_shape=[
            jax.ShapeDtypeStruct((n, RET_WIDTH), BF16),
            jax.ShapeDtypeStruct((n // tm, SSM_WIDTH * (tm // CHUNK), CHUNK), F32),
        ] + [jax.ShapeDtypeStruct(w.shape, BF16) for w in later],
        scratch_shapes=[pltpu.VMEM((RET_HEADS, HEAD_DIM, HEAD_DIM), F32),
                        pltpu.VMEM((D_MODEL, 4 * RET_WIDTH + SSM_WIDTH), BF16),
                        pltpu.VMEM((RET_WIDTH + SSM_WIDTH, D_MODEL), BF16)],
        compiler_params=pltpu.CompilerParams(
            dimension_semantics=("arbitrary", "arbitrary"), vmem_limit_bytes=VMEM_LIMIT_BYTES),
        name="mix_in",
    )(x2, g_pre, winf, cosf, sinf, cost, sint, xiq, ztt, gch, gn, *later)


def _s5_kernel(u_ref, pwk_ref, bc_ref, tab_ref, y_ref, sm_ref, wtap_ref, wst_ref, wct_ref, ktoe_ref, *, nc, nlev):
    m = u_ref.shape[0] * u_ref.shape[2]
    npair = SSM_GROUP // 2

    def smrow(base, i):
        return sm_ref[base + i:base + i + 1, :]

    def tabrow(base, i):
        return tab_ref[0, base + i:base + i + 1, :]

    def lo_half(shape):
        return lax.broadcasted_iota(jnp.int32, shape, 1) < HALF

    def swap(x):
        return pltpu.roll(x, HALF, 1)

    br0, bi0, cr0, ci0 = (bc_ref[0, k * SSM_GROUP:(k + 1) * SSM_GROUP, :] for k in range(4))
    sm_ref[ROW_BRR:ROW_BRR + SSM_GROUP, :] = br0 + swap(br0)
    sm_ref[ROW_BII:ROW_BII + SSM_GROUP, :] = bi0 + swap(bi0)
    sm_ref[ROW_BB1:ROW_BB1 + SSM_GROUP, :] = br0 - swap(bi0)
    sm_ref[ROW_BB2:ROW_BB2 + SSM_GROUP, :] = -bi0 - swap(br0)
    sm_ref[ROW_RC1:ROW_RC1 + SSM_GROUP, :] = cr0 + swap(cr0)
    sm_ref[ROW_RC2:ROW_RC2 + SSM_GROUP, :] = ci0 + swap(ci0)

    bb1 = sm_ref[ROW_BB1:ROW_BB1 + SSM_GROUP, :]
    bb2 = sm_ref[ROW_BB2:ROW_BB2 + SSM_GROUP, :]
    mc = lambda co: bb1 * smrow(ROW_RC1, co) + bb2 * smrow(ROW_RC2, co)
    mc_even = jnp.concatenate([mc(2 * cq) for cq in range(npair)], axis=0)
    mc_odd = jnp.concatenate([mc(2 * cq + 1) for cq in range(npair)], axis=0)
    pwk = pwk_ref[0]
    lo_sq = lo_half((2 * SSM_STATE, CHUNK))
    pw_lo = jnp.where(lo_sq, pwk, 0.0)
    pw_hi = jnp.where(lo_sq, 0.0, swap(pwk))
    wtap_ref[...] = _dot_3pass(mc_even, pw_lo) + _dot_3pass(mc_odd, pw_hi)

    sgn = jnp.where(lo_half((HALF, LANES)), -1.0, 1.0)
    pwt = pwk.T[:HALF, :]
    rev = (lax.broadcasted_iota(jnp.int32, (HALF, HALF), 0)
           + lax.broadcasted_iota(jnp.int32, (HALF, HALF), 1)) == HALF - 1
    a1 = _dot_3pass(jnp.where(rev, 1.0, 0.0), pwt)
    a2 = swap(a1) * sgn
    for ci in range(SSM_GROUP):
        wst_ref[ci * HALF:(ci + 1) * HALF, :] = (
            a1 * smrow(ROW_BRR, ci) + a2 * smrow(ROW_BII, ci)).astype(BF16)
    lam_row = pwt[1:2, :]
    lo_row = lo_half((1, LANES))
    lam_r = jnp.where(lo_row, lam_row, swap(lam_row))
    lam_i = jnp.where(lo_row, -swap(lam_row), lam_row)
    p1t = pwt * lam_r + swap(pwt) * lam_i
    b1 = p1t * (-sgn)
    b2 = -swap(p1t)
    for co in range(SSM_GROUP):
        wct_ref[co * HALF:(co + 1) * HALF, :] = (
            b1 * smrow(ROW_RC1, co) + b2 * smrow(ROW_RC2, co)).astype(BF16)

    def cmul(x, xs, base, i):
        lr, li = tabrow(base, i), tabrow(base, i + 1)
        return x * lr + xs * li, xs * lr - x * li

    nt, ncs = u_ref.shape[0], u_ref.shape[2]
    parts = S5_ROW_PARTS if (m // S5_ROW_PARTS) % nc == 0 else 1
    tp, mp = nt // parts, m // parts
    lo_m = lo_half((mp, LANES))
    block_id = lax.broadcasted_iota(jnp.int32, (mp, LANES), 0) % nc
    us, a_s, xprevs = [], [], []
    for pt_i in range(parts):
        u = [u_ref[pt_i * tp:(pt_i + 1) * tp, ci].reshape(mp, CHUNK) for ci in range(SSM_GROUP)]
        a_even = [jnp.where(lo_m, u[2 * cp], swap(u[2 * cp + 1])).astype(BF16) for cp in range(npair)]
        a_odd = [jnp.where(lo_m, swap(u[2 * cp]), u[2 * cp + 1]).astype(BF16) for cp in range(npair)]
        a = jnp.concatenate([jnp.concatenate(a_even, axis=1), jnp.concatenate(a_odd, axis=1)], axis=0)
        st = _dot(a, wst_ref[...])
        s_even, s_odd = st[:mp, :], st[mp:, :]

        se_s = swap(s_even)
        st, sts = cmul(s_even, se_s, TAB_L64, 0)
        st, sts = st + s_odd, sts + swap(s_odd)
        for lev in range(nlev):
            k = 1 << lev
            keep = block_id >= k
            inc, incs = cmul(jnp.where(keep, pltpu.roll(st, k, 0), 0.0),
                             jnp.where(keep, pltpu.roll(sts, k, 0), 0.0), TAB_LAM, 2 * lev)
            st, sts = st + inc, sts + incs
        first = block_id >= 1
        xprev_even = jnp.where(first, pltpu.roll(st, 1, 0), 0.0)
        xprev_even_s = jnp.where(first, pltpu.roll(sts, 1, 0), 0.0)
        xprev_odd = cmul(xprev_even, xprev_even_s, TAB_L64, 0)[0] + s_even
        us.append(u)
        a_s.append(a)
        xprevs.append(jnp.concatenate([xprev_even, xprev_odd], axis=0).astype(BF16))

    hrow = lax.broadcasted_iota(jnp.int32, (HALF, CHUNK), 0)
    hcol = lax.broadcasted_iota(jnp.int32, (HALF, CHUNK), 1) % HALF
    valid = hcol >= hrow
    nblk = MXU_DIM // CHUNK
    for j in range(npair // nblk):
        for cc in range(nblk):
            cq = j * nblk + cc
            for ci in range(SSM_GROUP):
                tile = jnp.broadcast_to(wtap_ref[cq * SSM_GROUP + ci:cq * SSM_GROUP + ci + 1, :], (HALF, CHUNK))
                tile = pltpu.roll(tile, 0, 1, stride=1, stride_axis=0)
                ktoe_ref[j, ci * HALF:(ci + 1) * HALF, cc * CHUNK:(cc + 1) * CHUNK] = (
                    jnp.where(valid, tile, 0.0).astype(BF16))
        for pt_i in range(parts):
            yj = (_dot(a_s[pt_i], ktoe_ref[j])
                  + _dot_nt(xprevs[pt_i], wct_ref[j * MXU_DIM:(j + 1) * MXU_DIM, :]))
            for cc in range(nblk):
                cq = j * nblk + cc
                y_even = yj[:mp, cc * CHUNK:(cc + 1) * CHUNK]
                y_odd = yj[mp:, cc * CHUNK:(cc + 1) * CHUNK]
                outs = (jnp.where(lo_m, y_even, swap(y_odd)), jnp.where(lo_m, swap(y_even), y_odd))
                for i in range(2):
                    co = 2 * cq + i
                    yc = outs[i] + tabrow(TAB_DSK, co) * us[pt_i][co]
                    y_ref[pt_i * tp:(pt_i + 1) * tp, co] = jax.nn.gelu(yc).reshape(tp, ncs, CHUNK)


def _s5(ut4, pwk, bc, tab, nc, nlev):
    nt, _, ncs, _ = ut4.shape
    grp = lambda g: (g, 0, 0)
    ugrp = lambda g: (0, g, 0, 0)
    return pl.pallas_call(
        functools.partial(_s5_kernel, nc=nc, nlev=nlev),
        grid=(SSM_GROUPS,),
        in_specs=[
            pl.BlockSpec((nt, SSM_GROUP, ncs, CHUNK), ugrp),
            pl.BlockSpec((1, 2 * SSM_STATE, CHUNK), grp),
            pl.BlockSpec((1, 4 * SSM_GROUP, LANES), grp),
            pl.BlockSpec((1, TAB_ROWS, LANES), grp),
        ],
        out_specs=pl.BlockSpec((nt, SSM_GROUP, ncs, CHUNK), ugrp),
        out_shape=jax.ShapeDtypeStruct(ut4.shape, F32),
        scratch_shapes=[
            pltpu.VMEM((SM_ROWS, LANES), F32),
            pltpu.VMEM((SSM_GROUP * SSM_GROUP // 2, CHUNK), F32),
            pltpu.VMEM((SSM_GROUP * HALF, 2 * SSM_STATE), BF16),
            pltpu.VMEM((SSM_GROUP * HALF, 2 * SSM_STATE), BF16),
            pltpu.VMEM((SSM_GROUP * HALF // MXU_DIM, SSM_GROUP * HALF, MXU_DIM), BF16),
        ],
        compiler_params=pltpu.CompilerParams(
            dimension_semantics=("parallel",), vmem_limit_bytes=VMEM_LIMIT_BYTES),
        name="s5",
    )(ut4, pwk, bc, tab)


def _mlp_kernel(x_ref, yret_ref, yst_ref, wglu_ref, wout_ref, gpost_ref, gpre2_ref, gpost2_ref,
                w1_ref, w2_ref, o_ref, *, ff_chunk, row_splits):
    tm = x_ref.shape[0]
    ncs = tm // CHUNK
    rs, cps = tm // row_splits, ncs // row_splits
    x1s, hs = [], []
    for sp in range(row_splits):
        rows = slice(sp * rs, (sp + 1) * rs)
        yst = jnp.concatenate([yst_ref[0, pl.ds(c, SSM_WIDTH, stride=ncs), :].astype(BF16)
                               for c in range(sp * cps, (sp + 1) * cps)], axis=1)
        glu = _dot_tn(yst, wglu_ref[...])
        yssm = (glu[:, :SSM_WIDTH] * jax.nn.sigmoid(glu[:, SSM_WIDTH:])).astype(BF16)
        mix = _dot(yret_ref[rows, :], wout_ref[:RET_WIDTH, :]) + _dot(yssm, wout_ref[RET_WIDTH:, :])
        x1 = x_ref[rows, :] + _rms(mix, gpost_ref[...])
        x1s.append(x1)
        hs.append(_rms(x1, gpre2_ref[...]).astype(BF16))
    mms = [None] * row_splits
    for c in range(D_FF // ff_chunk):
        for sp in range(row_splits):
            f = jnp.maximum(_dot(hs[sp], w1_ref[:, c * ff_chunk:(c + 1) * ff_chunk]), 0.0)
            part = _dot((f * f).astype(BF16), w2_ref[c * ff_chunk:(c + 1) * ff_chunk, :])
            mms[sp] = part if mms[sp] is None else mms[sp] + part
    for sp in range(row_splits):
        o_ref[sp * rs:(sp + 1) * rs, :] = x1s[sp] + _rms(mms[sp], gpost2_ref[...])


def _mlp(x2, yret, yst, wglu, wout, gpost, gpre2, gpost2, w1, w2, tm, ff_chunk):
    n = x2.shape[0]
    const = lambda i: (0, 0)
    once = pl.Buffered(1)
    return pl.pallas_call(
        functools.partial(_mlp_kernel, ff_chunk=ff_chunk, row_splits=MLP_ROW_SPLITS),
        grid=(n // tm,),
        in_specs=[
            pl.BlockSpec((tm, D_MODEL), lambda i: (i, 0)),
            pl.BlockSpec((tm, RET_WIDTH), lambda i: (i, 0)),
            pl.BlockSpec((1, SSM_WIDTH * (tm // CHUNK), CHUNK), lambda i: (i, 0, 0)),
            pl.BlockSpec((SSM_WIDTH, 2 * SSM_WIDTH), const, pipeline_mode=once),
            pl.BlockSpec((D_MODEL, D_MODEL), const, pipeline_mode=once),
            pl.BlockSpec((1, D_MODEL), const),
            pl.BlockSpec((1, D_MODEL), const),
            pl.BlockSpec((1, D_MODEL), const),
            pl.BlockSpec((D_MODEL, D_FF), const, pipeline_mode=once),
            pl.BlockSpec((D_FF, D_MODEL), const, pipeline_mode=once),
        ],
        out_specs=pl.BlockSpec((tm, D_MODEL), lambda i: (i, 0)),
        out_shape=jax.ShapeDtypeStruct((n, D_MODEL), F32),
        compiler_params=pltpu.CompilerParams(
            dimension_semantics=("parallel",), vmem_limit_bytes=VMEM_LIMIT_BYTES),
        name="mlp",
    )(x2, yret, yst, wglu, wout, gpost, gpre2, gpost2, w1, w2)


def _rope_tables(seq_len):
    half = HEAD_DIM // 2
    inv_freq = ROPE_BASE ** (-np.arange(half, dtype=np.float64) / half)
    ang = np.arange(seq_len, dtype=np.float64)[:, None] * inv_freq[None, :]
    cos, sin = np.cos(ang), np.sin(ang)
    cosf = np.concatenate([cos, cos], axis=1)
    sinf = np.concatenate([-sin, sin], axis=1)
    kscale = HEAD_DIM ** -0.5
    f32 = lambda t: jnp.asarray(np.ascontiguousarray(t), dtype=F32)
    return f32(cosf), f32(sinf), f32(cosf.T * kscale), f32(sinf.T * kscale)


def _retention_tables(tm):
    log_gamma = np.log(1.0 - np.exp(np.linspace(math.log(1.0 / 32), math.log(1.0 / 512), RET_HEADS)))
    idx = np.arange(CHUNK, dtype=np.float64)
    xiq = np.exp((idx + 1.0 - CHUNK)[None, :] * log_gamma[:, None])
    zeta = np.exp((CHUNK - 1 - idx)[None, :] * log_gamma[:, None])
    gch = np.exp(CHUNK * log_gamma)
    xiq_b = np.broadcast_to(xiq[:, :, None], (RET_HEADS, CHUNK, HEAD_DIM))
    ztt = np.tile(zeta, (1, tm // CHUNK))[:, None, :]
    gch_b = np.broadcast_to(gch[:, None, None], (RET_HEADS, 1, HEAD_DIM))
    f32 = lambda t: jnp.asarray(np.ascontiguousarray(t), dtype=F32)
    return f32(xiq_b), f32(ztt), f32(gch_b)


def _s5_tables(lam_re, lam_im, log_dt, b_re, b_im, c_re, c_im, d_skip, nlev):
    assert nlev <= MAX_SCAN_LEVELS
    a = jnp.minimum(lam_re.astype(F32), -1e-4)
    b = lam_im.astype(F32)
    dt = jnp.exp(log_dt.astype(F32))[:, None]
    tau = jnp.arange(HALF, dtype=F32)
    mag = jnp.exp((a * dt)[:, :, None] * tau)
    ph = (b * dt)[:, :, None] * tau
    pr, pi = mag * jnp.cos(ph), mag * jnp.sin(ph)
    lr1, li = pr[:, :, 1] - 1.0, pi[:, :, 1]
    den = a * a + b * b
    c0r, c0i = (lr1 * a + li * b) / den, (li * a - lr1 * b) / den
    bre, bim = b_re.astype(F32), b_im.astype(F32)
    br = jnp.swapaxes(c0r[:, :, None] * bre - c0i[:, :, None] * bim, 1, 2)
    bi = jnp.swapaxes(c0r[:, :, None] * bim + c0i[:, :, None] * bre, 1, 2)
    cr, ci = c_re.astype(F32), c_im.astype(F32)

    pwk = jnp.concatenate([pr, pi], axis=1)
    pwk = jnp.pad(pwk, ((0, 0), (0, 0), (0, LANES - HALF)))
    cat = lambda x, y: jnp.concatenate([x, y], axis=-1)

    lam_rows = []
    for step in [float(CHUNK * (1 << lev)) for lev in range(MAX_SCAN_LEVELS)] + [float(HALF)]:
        mg = jnp.exp(a * dt * step)
        re, im = mg * jnp.cos(b * dt * step), mg * jnp.sin(b * dt * step)
        lam_rows += [cat(re, re), cat(-im, im)]
    dsk = jnp.broadcast_to(d_skip.astype(F32).reshape(SSM_GROUPS, SSM_GROUP, 1), (SSM_GROUPS, SSM_GROUP, LANES))
    tab = jnp.concatenate([jnp.stack(lam_rows, axis=1), dsk], axis=1)
    bc = jnp.concatenate([br, bi, cr, ci], axis=1)
    bc = jnp.pad(bc, ((0, 0), (0, 0), (0, LANES - SSM_STATE)))
    return pwk, bc, tab


def _tiles(seq_len):
    tm_mix = min(1024, seq_len)
    tm_mlp = tm_mix
    ff_chunk = 1024
    assert seq_len % CHUNK == 0 and seq_len % tm_mix == 0
    return tm_mix, tm_mlp, ff_chunk


def kernel(x, norm_mix_pre, norm_mix_post, w_in, ret_gn_gain, ssm_lambda_re, ssm_lambda_im, ssm_log_dt,
           ssm_b_re, ssm_b_im, ssm_c_re, ssm_c_im, ssm_d, w_glu, w_out, norm_mlp_pre, norm_mlp_post,
           w_ff1, w_ff2):
    batch, seq_len, _ = x.shape
    depth = w_in.shape[0]
    n = batch * seq_len
    nc = seq_len // CHUNK
    nlev = max(1, (nc - 1).bit_length())
    tm_mix, tm_mlp, ff_chunk = _tiles(seq_len)

    cosf, sinf, cost, sint = _rope_tables(seq_len)
    xiq, ztt, gch = _retention_tables(tm_mix)
    x2 = x.reshape(n, D_MODEL)
    for i in range(depth):
        gn = ret_gn_gain[i].astype(F32).reshape(RET_HEADS, 1, HEAD_DIM)
        yret, ut, wglu, wout, w1, w2 = _mix_in(
            x2, norm_mix_pre[i][None, :], w_in[i], cosf, sinf, cost, sint, xiq, ztt, gch, gn,
            w_glu[i], w_out[i], w_ff1[i], w_ff2[i], batch, seq_len, tm_mix)
        pwk, bc, tab = _s5_tables(
            ssm_lambda_re[i], ssm_lambda_im[i], ssm_log_dt[i], ssm_b_re[i], ssm_b_im[i],
            ssm_c_re[i], ssm_c_im[i], ssm_d[i], nlev)
        ncs = tm_mix // CHUNK
        yst = _s5(ut.reshape(n // tm_mix, SSM_WIDTH, ncs, CHUNK), pwk, bc, tab, nc, nlev)
        yst = yst.reshape(n // tm_mlp, SSM_WIDTH * ncs, CHUNK)
        x2 = _mlp(x2, yret, yst, wglu, wout,
                  norm_mix_post[i][None, :], norm_mlp_pre[i][None, :], norm_mlp_post[i][None, :],
                  w1, w2, tm_mlp, ff_chunk)
    return x2.reshape(batch, seq_len, D_MODEL)
```
